```python
import math
import jax, jax.numpy as jnp
from jax import lax
import numpy as np

D_MODEL = 1024
BATCH = 4
SEQ = 4096
DEPTH = 4
DEC_BATCH = 32
DEC_SEQ = 4
PAST_LEN = 8192
PAGE_SIZE = 128

N_MIXERS = 3
N_DN = (DEPTH + 2) // 3
N_SB = (DEPTH + 1) // 3
N_RW = DEPTH // 3
RMS_EPS = 1e-6

DN_HEADS = 8
DN_DK = D_MODEL // DN_HEADS
DN_DV = DN_DK
DN_WIDTH = DN_HEADS * DN_DK
DN_CONV = 4
DN_CHUNK = 64

SB_HEADS = 16
SB_DH = D_MODEL // SB_HEADS
SB_WIDTH = SB_HEADS * SB_DH
SB_BLOCK = 128

RW_HEAD = 64
RW_HEADS = D_MODEL // RW_HEAD
RW_DECAY_LORA = 64
RW_AAA_LORA = 64
RW_GATE_LORA = 128
RW_GN_EPS = 64e-5

PEER_HEADS = 8
PEER_NKEYS = 128
PEER_EXPERTS = PEER_NKEYS * PEER_NKEYS
PEER_QDIM = 128
PEER_HALF = PEER_QDIM // 2
PEER_TOPK = 16
PEER_BLOCK = 128

kernel_name = 'hybrid_deltanet_stickbreak_rwkv7_peer_step'


def rmsnorm(x, g):
    x32 = x.astype(jnp.float32)
    y = x32 * lax.rsqrt(jnp.mean(x32 * x32, axis=-1, keepdims=True) + RMS_EPS)
    return (y * g.astype(jnp.float32)).astype(x.dtype)


def l2norm(x, eps=1e-6):
    return x * lax.rsqrt(jnp.sum(x * x, axis=-1, keepdims=True) + eps)


def pad_time(a, pad):
    return jnp.pad(a, [(0, 0), (0, pad)] + [(0, 0)] * (a.ndim - 2))


def gated_delta_rule(q, k, v, g, beta, S0):
    B, T, H, DK = q.shape
    DV = v.shape[-1]
    C = DN_CHUNK
    pad = (-T) % C
    q, k, v, g, beta = (pad_time(t, pad) for t in (q, k, v, g, beta))
    N = (T + pad) // C

    def to_chunks(t):
        return jnp.moveaxis(t.reshape((B, N, C) + t.shape[2:]), 3, 1)

    q, k, v, g, beta = (to_chunks(t) for t in (q, k, v, g, beta))
    gc = jnp.cumsum(g, axis=-1)
    incl = jnp.tril(jnp.ones((C, C), bool))
    strict = jnp.tril(jnp.ones((C, C), bool), -1)
    decay = jnp.exp(jnp.where(incl, gc[..., :, None] - gc[..., None, :], -jnp.inf))
    m = jnp.where(strict, beta[..., None] * jnp.einsum('bhncd,bhnsd->bhncs', k, k) * decay, 0.0)
    rhs = jnp.concatenate([beta[..., None] * v, (beta * jnp.exp(gc))[..., None] * k], axis=-1)
    sol = lax.linalg.triangular_solve(m + jnp.eye(C, dtype=m.dtype), rhs, left_side=True, lower=True)
    u, w = sol[..., :DV], sol[..., DV:]
    qk = jnp.einsum('bhncd,bhnsd->bhncs', q, k) * decay
    q_dec = q * jnp.exp(gc)[..., None]
    k_dec = k * jnp.exp(gc[..., -1:] - gc)[..., None]
    g_last = jnp.exp(gc[..., -1])

    def step(S, inp):
        u_n, w_n, qk_n, qd_n, kd_n, gl_n = inp
        v_new = u_n - jnp.einsum('bhck,bhkv->bhcv', w_n, S)
        o_n = jnp.einsum('bhck,bhkv->bhcv', qd_n, S) + jnp.einsum('bhcs,bhsv->bhcv', qk_n, v_new)
        S = S * gl_n[..., None, None] + jnp.einsum('bhck,bhcv->bhkv', kd_n, v_new)
        return S, o_n

    xs = tuple(jnp.moveaxis(t, 2, 0) for t in (u, w, qk, q_dec, k_dec, g_last))
    S, o = lax.scan(step, S0, xs)
    o = jnp.transpose(o, (1, 0, 3, 2, 4)).reshape(B, N * C, H, DV)[:, :T]
    return o, S


def delta_mixer(h, conv0, S0, w_in, conv_w, A_log, dt_bias, o_norm, w_out):
    B, T, _ = h.shape
    f32 = jnp.float32
    proj = (h @ w_in).astype(f32)
    W3 = 3 * DN_WIDTH
    qkv = proj[..., :W3]
    z = proj[..., W3:W3 + DN_WIDTH]
    a_in = proj[..., W3 + DN_WIDTH:W3 + DN_WIDTH + DN_HEADS]
    b_in = proj[..., W3 + DN_WIDTH + DN_HEADS:]
    xcat = jnp.concatenate([conv0.astype(f32), qkv], axis=1)
    conv = sum(xcat[:, i:i + T] * conv_w[i] for i in range(DN_CONV))
    qkv = jax.nn.silu(conv)
    q = l2norm(qkv[..., :DN_WIDTH].reshape(B, T, DN_HEADS, DN_DK)) * (DN_DK ** -0.5)
    k = l2norm(qkv[..., DN_WIDTH:2 * DN_WIDTH].reshape(B, T, DN_HEADS, DN_DK))
    v = qkv[..., 2 * DN_WIDTH:].reshape(B, T, DN_HEADS, DN_DV)
    g = -jnp.exp(A_log.astype(f32)) * jax.nn.softplus(a_in + dt_bias)
    beta = jax.nn.sigmoid(b_in)
    o, S = gated_delta_rule(q, k, v, g, beta, S0.astype(f32))
    o = o * lax.rsqrt(jnp.mean(o * o, axis=-1, keepdims=True) + RMS_EPS) * o_norm
    o = o * jax.nn.silu(z.reshape(B, T, DN_HEADS, DN_DV))
    out = o.reshape(B, T, DN_WIDTH) @ w_out
    return out.astype(h.dtype), xcat[:, -(DN_CONV - 1):].astype(h.dtype), S.astype(h.dtype)


def stick_breaking_block(q, k, v, bias, q_pos0):
    z = jnp.einsum('bqhd,bshd->bhqs', q, k) * (q.shape[-1] ** -0.5) + bias[None, :, None, None]
    t_pos = q_pos0 + jnp.arange(q.shape[1])
    mask = jnp.arange(k.shape[1])[None, :] < t_pos[:, None]
    sp = jnp.where(mask, jax.nn.softplus(z), 0.0)
    later = lax.cumsum(sp, axis=3, reverse=True) - sp
    a = jnp.where(mask, jnp.exp(jax.nn.log_sigmoid(z) - later), 0.0)
    return jnp.einsum('bhqs,bshd->bqhd', a, v)


def sb_mixer(h, k_past, v_past, w_qkv, bias, w_out):
    B, T, _ = h.shape
    f32 = jnp.float32
    qkv = (h @ w_qkv).astype(f32).reshape(B, T, 3, SB_HEADS, SB_DH)
    q, k, v = qkv[:, :, 0], qkv[:, :, 1], qkv[:, :, 2]
    bias = bias.astype(f32)
    if k_past is None:
        k_all, v_all, pos0 = k, v, 0
    else:
        k_all = jnp.concatenate([k_past.astype(f32), k], axis=1)
        v_all = jnp.concatenate([v_past.astype(f32), v], axis=1)
        pos0 = k_past.shape[1]
    outs = []
    for q0 in range(0, T, SB_BLOCK):
        q1 = min(q0 + SB_BLOCK, T)
        kend = pos0 + q1
        outs.append(stick_breaking_block(q[:, q0:q1], k_all[:, :kend], v_all[:, :kend], bias, pos0 + q0))
    o = jnp.concatenate(outs, axis=1).reshape(B, T, SB_WIDTH)
    return (o @ w_out).astype(h.dtype), k.astype(h.dtype), v.astype(h.dtype)


def rwkv7_recurrence(r, decay, k, v, a_vec, b_vec, S0):
    def step(S, inp):
        r_t, d_t, k_t, v_t, a_t, b_t = inp
        S = (S * d_t[:, :, None, :]
             + jnp.einsum('bhvk,bhk->bhv', S, a_t)[..., None] * b_t[:, :, None, :]
             + v_t[..., None] * k_t[:, :, None, :])
        return S, jnp.einsum('bhvk,bhk->bhv', S, r_t)

    xs = tuple(jnp.moveaxis(t, 1, 0) for t in (r, decay, k, v, a_vec, b_vec))
    S, y = lax.scan(step, S0, xs)
    return jnp.moveaxis(y, 0, 1), S


def rwkv7_mixer(h, shift0, S0, mu, w_r, w_k, w_v, w0, w1, w2, a0, a1, a2, g1, g2,
                k_k, k_a, r_k, lnx_w, lnx_b, w_out):
    B, T, D = h.shape
    f32 = jnp.float32
    hf = h.astype(f32)
    prev = jnp.concatenate([shift0[:, None].astype(f32), hf[:, :-1]], axis=1)
    xx = prev - hf
    xr, xw, xk, xv, xa, xg = (hf + xx * mu[m] for m in range(6))
    r = xr @ w_r
    k = xk @ w_k
    v = xv @ w_v
    w = -jax.nn.softplus(-(w0 + jnp.tanh(xw @ w1) @ w2)) - 0.5
    decay = jnp.exp(-jnp.exp(w))
    a = jax.nn.sigmoid(a0 + (xa @ a1) @ a2)
    g = jax.nn.sigmoid(xg @ g1) @ g2

    def heads(t):
        return t.reshape(B, T, RW_HEADS, RW_HEAD)

    kk = l2norm(heads(k * k_k))
    k = heads(k * (1.0 + (a - 1.0) * k_a))
    r, v, a, decay = heads(r), heads(v), heads(a), heads(decay)
    y, S = rwkv7_recurrence(r, decay, k, v, -kk, kk * a, S0.astype(f32))
    mean = jnp.mean(y, axis=-1, keepdims=True)
    var = jnp.mean(jnp.square(y - mean), axis=-1, keepdims=True)
    y = ((y - mean) * lax.rsqrt(var + RW_GN_EPS)).reshape(B, T, D) * lnx_w + lnx_b
    y = y + (jnp.sum(r * k * r_k, axis=-1, keepdims=True) * v).reshape(B, T, D)
    out = (y * g) @ w_out
    return out.astype(h.dtype), h[:, -1], S.astype(h.dtype)


def peer_ffn(h, w_q, subkeys, u_tab, v_tab):
    B, T, D = h.shape
    xf = h.reshape(B * T, D)
    n = xf.shape[0]
    q = (xf @ w_q).astype(jnp.float32).reshape(n, PEER_HEADS, 2, PEER_HALF)
    s = jnp.einsum('nhpd,hpkd->nhpk', q, subkeys.astype(jnp.float32))
    s1, i1 = lax.top_k(s[:, :, 0], PEER_TOPK)
    s2, i2 = lax.top_k(s[:, :, 1], PEER_TOPK)
    cand = (s1[..., :, None] + s2[..., None, :]).reshape(n, PEER_HEADS, PEER_TOPK * PEER_TOPK)
    best, bi = lax.top_k(cand, PEER_TOPK)
    e1 = jnp.take_along_axis(i1, bi // PEER_TOPK, axis=-1)
    e2 = jnp.take_along_axis(i2, bi % PEER_TOPK, axis=-1)
    experts = e1 * PEER_NKEYS + e2
    gates = jax.nn.softmax(best, axis=-1)
    pad = (-n) % PEER_BLOCK
    nb = (n + pad) // PEER_BLOCK
    xb = jnp.pad(xf, ((0, pad), (0, 0))).reshape(nb, PEER_BLOCK, D)
    eb = jnp.pad(experts, ((0, pad), (0, 0), (0, 0))).reshape(nb, PEER_BLOCK, PEER_HEADS, PEER_TOPK)
    gb = jnp.pad(gates, ((0, pad), (0, 0), (0, 0))).reshape(nb, PEER_BLOCK, PEER_HEADS, PEER_TOPK)

    def expert_block(args):
        xs, es, gs = args
        act = jax.nn.gelu(jnp.einsum('nd,nhkd->nhk', xs, u_tab[es]).astype(jnp.float32), approximate=False)
        return jnp.einsum('nhk,nhkd->nd', gs * act, v_tab[es].astype(jnp.float32))

    out = lax.map(expert_block, (xb, eb, gb)).reshape(nb * PEER_BLOCK, D)[:n]
    return out.reshape(B, T, D).astype(h.dtype)


def trunk(x, dn_state, dn_conv, sb_k_past, sb_v_past, rw_state, rw_shift,
          norms, dn_p, sb_p, rw_p, peer_p):
    ln_mix, ln_ffn, ln_final = norms
    n_dn_s, n_dn_c, n_k, n_v, n_rw_s, n_rw_x = [], [], [], [], [], []
    for i in range(DEPTH):
        j = i // N_MIXERS
        kind = i % N_MIXERS
        h = rmsnorm(x, ln_mix[i])
        if kind == 0:
            mix, c, s = delta_mixer(h, dn_conv[j], dn_state[j], *[p[j] for p in dn_p])
            n_dn_c.append(c)
            n_dn_s.append(s)
        elif kind == 1:
            kp = None if sb_k_past is None else sb_k_past[j]
            vp = None if sb_v_past is None else sb_v_past[j]
            mix, kr, vr = sb_mixer(h, kp, vp, *[p[j] for p in sb_p])
            n_k.append(kr)
            n_v.append(vr)
        else:
            mix, xs, s = rwkv7_mixer(h, rw_shift[j], rw_state[j], *[p[j] for p in rw_p])
            n_rw_x.append(xs)
            n_rw_s.append(s)
        x = x + mix
        x = x + peer_ffn(rmsnorm(x, ln_ffn[i]), *[p[i] for p in peer_p])
    return (rmsnorm(x, ln_final), jnp.stack(n_dn_s), jnp.stack(n_dn_c), jnp.stack(n_k),
            jnp.stack(n_v), jnp.stack(n_rw_s), jnp.stack(n_rw_x))


def setup_inputs(seed: int = 0) -> dict:
    key = jax.random.key(seed)
    keys = jax.random.split(key, 64)
    cnt = [0]

    def nk():
        cnt[0] += 1
        return keys[cnt[0] - 1]

    def nrm(shape, scale):
        return jax.random.normal(nk(), shape, jnp.float32) * scale

    def uni(shape, lo, hi):
        return jax.random.uniform(nk(), shape, jnp.float32, lo, hi)

    D = D_MODEL
    n_pages = PAST_LEN // PAGE_SIZE
    used = DEC_BATCH * n_pages
    n_pool = used + max(1, used // 4)
    page_table = jax.random.permutation(nk(), n_pool)[:used].reshape(DEC_BATCH, n_pages).astype(jnp.int32)
    dt = jnp.exp(uni((N_DN, DN_HEADS), math.log(1e-3), math.log(1e-1)))
    return {
        'x_prompt': nrm((BATCH, SEQ, D), 1.0),
        'x_sample': nrm((DEC_BATCH, DEC_SEQ, D), 1.0),
        'state_dn': nrm((N_DN, DEC_BATCH, DN_HEADS, DN_DK, DN_DV), 0.5),
        'state_dn_conv': nrm((N_DN, DEC_BATCH, DN_CONV - 1, 3 * DN_WIDTH), 1.0),
        'cache_k': nrm((N_SB, n_pool, PAGE_SIZE, SB_HEADS, SB_DH), 1.0),
        'cache_v': nrm((N_SB, n_pool, PAGE_SIZE, SB_HEADS, SB_DH), 1.0),
        'page_table': page_table,
        'state_wkv': nrm((N_RW, DEC_BATCH, RW_HEADS, RW_HEAD, RW_HEAD), 0.5),
        'state_shift': nrm((N_RW, DEC_BATCH, D), 1.0),
        'ln_mix': 1.0 + nrm((DEPTH, D), 0.02),
        'ln_ffn': 1.0 + nrm((DEPTH, D), 0.02),
        'ln_final': 1.0 + nrm((D,), 0.02),
        'dn_w_in': nrm((N_DN, D, 4 * DN_WIDTH + 2 * DN_HEADS), D ** -0.5),
        'dn_conv_w': nrm((N_DN, DN_CONV, 3 * DN_WIDTH), DN_CONV ** -0.5),
        'dn_A_log': jnp.log(uni((N_DN, DN_HEADS), 1.0, 16.0)),
        'dn_dt_bias': dt + jnp.log(-jnp.expm1(-dt)),
        'dn_o_norm': 1.0 + nrm((N_DN, DN_DV), 0.02),
        'dn_w_out': nrm((N_DN, DN_WIDTH, D), DN_WIDTH ** -0.5),
        'sb_w_qkv': nrm((N_SB, D, 3 * SB_WIDTH), D ** -0.5),
        'sb_bias': uni((N_SB, SB_HEADS), -10.0, -8.0),
        'sb_w_out': nrm((N_SB, SB_WIDTH, D), SB_WIDTH ** -0.5),
        'rw_mu': uni((N_RW, 6, D), 0.0, 1.0),
        'rw_w_r': nrm((N_RW, D, D), D ** -0.5),
        'rw_w_k': nrm((N_RW, D, D), D ** -0.5),
        'rw_w_v': nrm((N_RW, D, D), D ** -0.5),
        'rw_w0': uni((N_RW, D), -6.0, 0.0),
        'rw_w1': nrm((N_RW, D, RW_DECAY_LORA), D ** -0.5),
        'rw_w2': nrm((N_RW, RW_DECAY_LORA, D), 0.1 * RW_DECAY_LORA ** -0.5),
        'rw_a0': nrm((N_RW, D), 0.1),
        'rw_a1': nrm((N_RW, D, RW_AAA_LORA), D ** -0.5),
        'rw_a2': nrm((N_RW, RW_AAA_LORA, D), RW_AAA_LORA ** -0.5),
        'rw_g1': nrm((N_RW, D, RW_GATE_LORA), D ** -0.5),
        'rw_g2': nrm((N_RW, RW_GATE_LORA, D), RW_GATE_LORA ** -0.5),
        'rw_k_k': 0.85 + nrm((N_RW, D), 0.02),
        'rw_k_a': 1.0 + nrm((N_RW, D), 0.02),
        'rw_r_k': nrm((N_RW, RW_HEADS, RW_HEAD), 0.1),
        'rw_lnx_w': 1.0 + nrm((N_RW, D), 0.02),
        'rw_lnx_b': nrm((N_RW, D), 0.02),
        'rw_w_out': nrm((N_RW, D, D), D ** -0.5),
        'peer_w_q': nrm((DEPTH, D, PEER_HEADS * PEER_QDIM), D ** -0.5),
        'peer_subkeys': nrm((DEPTH, PEER_HEADS, 2, PEER_NKEYS, PEER_HALF), PEER_HALF ** -0.5),
        'peer_u': nrm((DEPTH, PEER_EXPERTS, D), D ** -0.5),
        'peer_v': nrm((DEPTH, PEER_EXPERTS, D), (PEER_HEADS * PEER_TOPK) ** -0.5),
    }


def reference(x_prompt, x_sample, state_dn, state_dn_conv, cache_k, cache_v, page_table,
              state_wkv, state_shift, ln_mix, ln_ffn, ln_final,
              dn_w_in, dn_conv_w, dn_A_log, dn_dt_bias, dn_o_norm, dn_w_out,
              sb_w_qkv, sb_bias, sb_w_out,
              rw_mu, rw_w_r, rw_w_k, rw_w_v, rw_w0, rw_w1, rw_w2, rw_a0, rw_a1, rw_a2,
              rw_g1, rw_g2, rw_k_k, rw_k_a, rw_r_k, rw_lnx_w, rw_lnx_b, rw_w_out,
              peer_w_q, peer_subkeys, peer_u, peer_v):
    norms = (ln_mix, ln_ffn, ln_final)
    dn_p = (dn_w_in, dn_conv_w, dn_A_log, dn_dt_bias, dn_o_norm, dn_w_out)
    sb_p = (sb_w_qkv, sb_bias, sb_w_out)
    rw_p = (rw_mu, rw_w_r, rw_w_k, rw_w_v, rw_w0, rw_w1, rw_w2, rw_a0, rw_a1, rw_a2,
            rw_g1, rw_g2, rw_k_k, rw_k_a, rw_r_k, rw_lnx_w, rw_lnx_b, rw_w_out)
    peer_p = (peer_w_q, peer_subkeys, peer_u, peer_v)

    b, dt = x_prompt.shape[0], x_prompt.dtype
    (y_prompt, p_dn, p_dn_conv, p_k, p_v, p_wkv, p_shift) = trunk(
        x_prompt,
        jnp.zeros((N_DN, b, DN_HEADS, DN_DK, DN_DV), dt),
        jnp.zeros((N_DN, b, DN_CONV - 1, 3 * DN_WIDTH), dt),
        None, None,
        jnp.zeros((N_RW, b, RW_HEADS, RW_HEAD, RW_HEAD), dt),
        jnp.zeros((N_RW, b, D_MODEL), dt),
        norms, dn_p, sb_p, rw_p, peer_p)

    db, n_pages = page_table.shape
    past = n_pages * cache_k.shape[2]
    k_past = cache_k[:, page_table].reshape(N_SB, db, past, SB_HEADS, SB_DH)
    v_past = cache_v[:, page_table].reshape(N_SB, db, past, SB_HEADS, SB_DH)
    (y_sample, s_dn, s_dn_conv, s_k, s_v, s_wkv, s_shift) = trunk(
        x_sample, state_dn, state_dn_conv, k_past, v_past, state_wkv, state_shift,
        norms, dn_p, sb_p, rw_p, peer_p)

    return (y_prompt, y_sample, p_dn, p_dn_conv, p_k, p_v, p_wkv, p_shift,
            s_dn, s_dn_conv, s_k, s_v, s_wkv, s_shift)
```

```python
import functools
import math

import jax
import jax.numpy as jnp
from jax import lax
from jax.experimental import pallas as pl
from jax.experimental.pallas import tpu as pltpu

F32 = jnp.float32
BF16 = jnp.bfloat16
HIGHEST = lax.Precision.HIGHEST

D_MODEL = 1024
RMS_EPS = 1e-6
LANES = 128
SUBLANES = 8
VMEM_LIMIT = 48 * 1024 * 1024

DN_HEADS = 8
DN_DK = 128
DN_WIDTH = 1024
DN_CONV = 4
DN_CHUNK = 64
DN_PROJ = 4 * DN_WIDTH + 2 * DN_HEADS
DN_PROJ_PAD = 4 * DN_WIDTH + LANES

SB_HEADS = 16
SB_DH = 64
PAGE_SIZE = 128

RW_HEAD = 64
RW_HEADS = 16
RW_GN_EPS = 64e-5
RW_CHUNK = 64

PEER_HEADS = 8
PEER_NKEYS = 128
PEER_HALF = 64
PEER_TOPK = 16
NEG_BIG = -3.0e38


def _dot(a, b, precision=None):
    return jnp.dot(a, b, preferred_element_type=F32, precision=precision)


def _dot_nt(a, b, precision=None):
    return lax.dot_general(a, b, (((1,), (1,)), ((), ())), preferred_element_type=F32,
                           precision=precision)


def _split(a):
    hi = a.astype(BF16)
    lo = (a - hi.astype(F32)).astype(BF16)
    return hi, lo


def _dot3(a, b):
    ah, al = _split(a)
    bh, bl = _split(b)
    return _dot(ah, bh) + (_dot(ah, bl) + _dot(al, bh))


def _dot3_nt(a, b):
    ah, al = _split(a)
    bh, bl = _split(b)
    return _dot_nt(ah, bh) + (_dot_nt(ah, bl) + _dot_nt(al, bh))


def _dot_tn(a, b):
    return lax.dot_general(a, b, (((0,), (0,)), ((), ())), preferred_element_type=F32)


def _dot3_tn(a, b):
    ah, al = _split(a)
    bh, bl = _split(b)
    return _dot_tn(ah, bh) + (_dot_tn(ah, bl) + _dot_tn(al, bh))


def _dotb_tn(a, b):
    return _dot_tn(a.astype(BF16), b.astype(BF16))


def _dotb(a, b):
    return _dot(a.astype(BF16), b.astype(BF16))


def _dotb_nt(a, b):
    return _dot_nt(a.astype(BF16), b.astype(BF16))


def _dot_exact_rhs(a, b_exact):
    ah, al = _split(a)
    return _dot(ah, b_exact) + _dot(al, b_exact)


def _rms(x, g):
    return x * lax.rsqrt(jnp.mean(x * x, axis=-1, keepdims=True) + RMS_EPS) * g


def _sigmoid(x):
    return 1.0 / (1.0 + jnp.exp(-x))


def _softplus(x):
    return jnp.maximum(x, 0.0) + jnp.log1p(jnp.exp(-jnp.abs(x)))


def _iota2(shape, axis):
    return lax.broadcasted_iota(jnp.int32, shape, axis)


def _tri_inv(m, c):
    ri = _iota2((c, c), 0)
    ci = _iota2((c, c), 1)
    eye = jnp.where(ri == ci, 1.0, 0.0).astype(F32)
    m8 = jnp.where((ri >> 3) == (ci >> 3), m, 0.0)
    m8_2 = _dot3(m8, m8)
    m8_4 = _dot3(m8_2, m8_2)
    x = _dot3(_dot3(eye - m8, eye + m8_2), eye + m8_4)
    size, shift = 8, 3
    while size < c:
        lowleft = jnp.where((ri >> (shift + 1)) == (ci >> (shift + 1)),
                            jnp.where(((ri >> shift) & 1) == 1,
                                      jnp.where(((ci >> shift) & 1) == 0, 1.0, 0.0), 0.0), 0.0)
        cm = m * lowleft
        x = x - _dot3(_dot3(x, cm), x)
        size, shift = size * 2, shift + 1
    return x


def _full(a):
    nd = a.ndim
    return pl.BlockSpec(a.shape, lambda *_, _nd=nd: (0,) * _nd)


def _cparams(sem):
    return pltpu.CompilerParams(dimension_semantics=sem, vmem_limit_bytes=VMEM_LIMIT)


def _sds(shape, dtype=F32):
    return jax.ShapeDtypeStruct(shape, dtype)


def _mm_res_body(a_ref, x_ref, w_ref, o_ref):
    o_ref[...] = x_ref[...] + _dot(a_ref[...].astype(BF16), w_ref[...])


def _mm_res(a, x, w_bf16, tm):
    n, k = a.shape
    m = w_bf16.shape[1]
    return pl.pallas_call(
        _mm_res_body, grid=(n // tm,),
        in_specs=[pl.BlockSpec((tm, k), lambda i: (i, 0)), pl.BlockSpec((tm, m), lambda i: (i, 0)),
                  _full(w_bf16)],
        out_specs=pl.BlockSpec((tm, m), lambda i: (i, 0)),
        out_shape=_sds((n, m)), compiler_params=_cparams(("parallel",)), name="mm_res",
    )(a, x, w_bf16)


def _final_norm_body(x_ref, g_ref, o_ref):
    o_ref[...] = _rms(x_ref[...], g_ref[...])


def _final_norm(x, g, tm):
    n = x.shape[0]
    return pl.pallas_call(
        _final_norm_body, grid=(n // tm,),
        in_specs=[pl.BlockSpec((tm, D_MODEL), lambda i: (i, 0)), _full(g)],
        out_specs=pl.BlockSpec((tm, D_MODEL), lambda i: (i, 0)),
        out_shape=_sds((n, D_MODEL)), compiler_params=_cparams(("parallel",)), name="final_norm",
    )(x, g)


def _dn_pre_body(x_ref, c0_ref, ln_ref, w_ref, cw_ref, pa_ref,
                 q_ref, k_ref, v_ref, z_ref, gc_ref, bt_ref, tail_ref,
                 proj_s, cbuf, *, tt, t_valid, nt, chunk):
    t = pl.program_id(1)
    h = _rms(x_ref[...], ln_ref[...])
    proj_s[...] = _dot(h.astype(BF16), w_ref[...])

    @pl.when(t == 0)
    def _():
        cbuf[0:SUBLANES, :] = c0_ref[0]

    cbuf[SUBLANES:SUBLANES + tt, :] = proj_s[:, 0:3 * DN_WIDTH]
    for cb in range(3 * DN_HEADS):
        sl = slice(cb * LANES, (cb + 1) * LANES)
        conv = (cbuf[5:5 + tt, sl] * cw_ref[0:1, sl] + cbuf[6:6 + tt, sl] * cw_ref[1:2, sl]
                + cbuf[7:7 + tt, sl] * cw_ref[2:3, sl] + cbuf[8:8 + tt, sl] * cw_ref[3:4, sl])
        a = conv * _sigmoid(conv)
        if cb < DN_HEADS:
            a = a * lax.rsqrt(jnp.sum(a * a, axis=-1, keepdims=True) + 1e-6) * (DN_DK ** -0.5)
            q_ref[:, sl] = a
        elif cb < 2 * DN_HEADS:
            a = a * lax.rsqrt(jnp.sum(a * a, axis=-1, keepdims=True) + 1e-6)
            k_ref[:, (cb - DN_HEADS) * LANES:(cb - DN_HEADS + 1) * LANES] = a
        else:
            v_ref[:, (cb - 2 * DN_HEADS) * LANES:(cb - 2 * DN_HEADS + 1) * LANES] = a
    z_ref[...] = proj_s[:, 3 * DN_WIDTH:4 * DN_WIDTH]

    last = proj_s[:, 4 * DN_WIDTH:DN_PROJ_PAD]
    g = -jnp.exp(pa_ref[0:1, :]) * _softplus(last + pa_ref[1:2, :])
    beta = _sigmoid(pltpu.roll(last, LANES - DN_HEADS, 1))
    row = t * tt + _iota2((tt, LANES), 0)
    lane = _iota2((tt, LANES), 1)
    keep = jnp.where(row < t_valid, jnp.where(lane < DN_HEADS, 1.0, 0.0), 0.0)
    g = g * keep
    bt_ref[...] = beta * keep
    ri = _iota2((tt, tt), 0)
    ci = _iota2((tt, tt), 1)
    sh = int(math.log2(chunk))
    ltri = jnp.where((ri >> sh) == (ci >> sh), jnp.where(ci <= ri, 1.0, 0.0), 0.0).astype(F32)
    gc_ref[...] = _dot(ltri, g, HIGHEST)

    tl = t_valid - (nt - 1) * tt

    @pl.when(t == nt - 1)
    def _():
        tail_ref[0] = cbuf[tl:tl + SUBLANES, :]

    cbuf[0:SUBLANES, :] = cbuf[tt:tt + SUBLANES, :]


def _dn_pre(x, conv0, ln, w_in, conv_w, pa, *, bsz, tp, tt, t_valid, chunk):
    nt = tp // tt
    n = bsz * tp
    row = lambda b, t: (b * nt + t, 0)
    body = functools.partial(_dn_pre_body, tt=tt, t_valid=t_valid, nt=nt, chunk=chunk)
    wide = pl.BlockSpec((tt, DN_WIDTH), row)
    narrow = pl.BlockSpec((tt, LANES), row)
    return pl.pallas_call(
        body, grid=(bsz, nt),
        in_specs=[wide, pl.BlockSpec((1, SUBLANES, 3 * DN_WIDTH), lambda b, t: (b, 0, 0)),
                  _full(ln), _full(w_in), _full(conv_w), _full(pa)],
        out_specs=[wide, wide, wide, wide, narrow, narrow,
                   pl.BlockSpec((1, SUBLANES, 3 * DN_WIDTH), lambda b, t: (b, 0, 0))],
        out_shape=[_sds((n, DN_WIDTH))] * 4 + [_sds((n, LANES))] * 2
                  + [_sds((bsz, SUBLANES, 3 * DN_WIDTH))],
        scratch_shapes=[pltpu.VMEM((tt, DN_PROJ_PAD), F32),
                        pltpu.VMEM((tt + 2 * SUBLANES, 3 * DN_WIDTH), F32)],
        compiler_params=_cparams(("parallel", "arbitrary")), name="dn_pre",
    )(x, conv0, ln, w_in, conv_w, pa)


def _dn_chunk_body(q_ref, k_ref, v_ref, gc_ref, bt_ref, s0_ref, o_ref, sout_ref, s_scr,
                   *, c, nc, fine):
    ch = pl.program_id(1)

    @pl.when(ch == 0)
    def _():
        s_scr[...] = s0_ref[0]

    mm = _dot3 if fine else _dotb
    mm_nt = _dot3_nt if fine else _dotb_nt
    mm_tn = _dot3_tn if fine else _dotb_tn
    ri = _iota2((c, c), 0)
    ci = _iota2((c, c), 1)
    for h in range(DN_HEADS):
        sl = slice(h * LANES, (h + 1) * LANES)
        q = q_ref[:, sl]
        k = k_ref[:, sl]
        v = v_ref[:, sl]
        gcol = gc_ref[:, h:h + 1]
        bcol = bt_ref[:, h:h + 1]
        gmat = jnp.broadcast_to(gcol, (c, c))
        grow = jnp.sum(jnp.where(ri == ci, gmat, 0.0), axis=0, keepdims=True)
        decay = jnp.exp(jnp.where(ci <= ri, gmat - grow, NEG_BIG))
        kk = mm_nt(k, k)
        m = jnp.where(ci < ri, bcol * kk * decay, 0.0)
        x = _tri_inv(m, c)
        eg = jnp.exp(gcol)
        u = _dot3(x, bcol * v)
        w = _dot3(x, (bcol * eg) * k)
        qk = mm_nt(q, k) * decay
        glast = gc_ref[c - 1:c, h:h + 1]
        s_h = s_scr[h]
        v_new = u - mm(w, s_h)
        o_ref[:, sl] = mm(q * eg, s_h) + mm(qk, v_new)
        k_dec = k * jnp.exp(glast - gcol)
        s_scr[h] = s_h * jnp.exp(glast) + mm_tn(k_dec, v_new)

    @pl.when(ch == nc - 1)
    def _():
        sout_ref[0] = s_scr[...]


def _dn_chunk(q, k, v, gc, bt, s0, *, bsz, tp, c, fine):
    nc = tp // c
    n = bsz * tp
    row = lambda b, j: (b * nc + j, 0)
    wide = pl.BlockSpec((c, DN_WIDTH), row)
    narrow = pl.BlockSpec((c, LANES), row)
    st = pl.BlockSpec((1, DN_HEADS, DN_DK, DN_DK), lambda b, j: (b, 0, 0, 0))
    body = functools.partial(_dn_chunk_body, c=c, nc=nc, fine=fine)
    return pl.pallas_call(
        body, grid=(bsz, nc),
        in_specs=[wide, wide, wide, narrow, narrow, st],
        out_specs=[wide, st],
        out_shape=[_sds((n, DN_WIDTH)), _sds((bsz, DN_HEADS, DN_DK, DN_DK))],
        scratch_shapes=[pltpu.VMEM((DN_HEADS, DN_DK, DN_DK), F32)],
        compiler_params=_cparams(("parallel", "arbitrary")), name="dn_chunk",
    )(q, k, v, gc, bt, s0)


def _dn_out_body(o_ref, z_ref, x_ref, on_ref, w_ref, y_ref):
    parts = []
    for h in range(DN_HEADS):
        sl = slice(h * LANES, (h + 1) * LANES)
        o = o_ref[:, sl]
        o = o * lax.rsqrt(jnp.mean(o * o, axis=-1, keepdims=True) + RMS_EPS) * on_ref[...]
        z = z_ref[:, sl]
        parts.append((o * (z * _sigmoid(z))).astype(BF16))
    y_ref[...] = x_ref[...] + _dot(jnp.concatenate(parts, axis=-1), w_ref[...])


def _dn_out(o, z, x, o_norm, w_out, tm):
    n = o.shape[0]
    blk = pl.BlockSpec((tm, D_MODEL), lambda i: (i, 0))
    return pl.pallas_call(
        _dn_out_body, grid=(n // tm,),
        in_specs=[blk, blk, blk, _full(o_norm), _full(w_out)],
        out_specs=blk, out_shape=_sds((n, D_MODEL)),
        compiler_params=_cparams(("parallel",)), name="dn_out",
    )(o, z, x, o_norm, w_out)


def _delta_layer(x, conv0, s0, p, *, bsz, tp, tt, t_valid, chunk, fine, tm):
    q, k, v, z, gc, bt, tail = _dn_pre(x, conv0, p["ln"], p["w_in"], p["conv_w"], p["pa"],
                                       bsz=bsz, tp=tp, tt=tt, t_valid=t_valid, chunk=chunk)
    o, s_new = _dn_chunk(q, k, v, gc, bt, s0, bsz=bsz, tp=tp, c=chunk, fine=fine)
    y = _dn_out(o, z, x, p["o_norm"], p["w_out"], tm)
    return y, tail[:, 5:8, :], s_new


def _sb_qkv_body(x_ref, ln_ref, w_ref, q_ref, k_ref, v_ref):
    h = _rms(x_ref[...], ln_ref[...]).astype(BF16)
    r = _dot(h, w_ref[...])
    q_ref[...] = r[:, 0:D_MODEL]
    k_ref[...] = r[:, D_MODEL:2 * D_MODEL]
    v_ref[...] = r[:, 2 * D_MODEL:3 * D_MODEL]


def _sb_qkv(x, ln, w, tm):
    n = x.shape[0]
    blk = pl.BlockSpec((tm, D_MODEL), lambda i: (i, 0))
    return pl.pallas_call(
        _sb_qkv_body, grid=(n // tm,),
        in_specs=[blk, _full(ln), _full(w)],
        out_specs=[blk, blk, blk], out_shape=[_sds((n, D_MODEL))] * 3,
        compiler_params=_cparams(("parallel",)), name="sb_qkv",
    )(x, ln, w)


def _later_matrix():
    j = _iota2((LANES, 2 * LANES), 0)
    s = _iota2((LANES, 2 * LANES), 1)
    return jnp.where(s >= LANES, 1.0, jnp.where(j > s, 1.0, 0.0)).astype(BF16)


def _sb_tile(z, mask, carry, uo):
    sp = _softplus(z)
    if mask is not None:
        sp = jnp.where(mask, sp, 0.0)
    cs = _dot_exact_rhs(sp, uo)
    a = jnp.exp(z - sp - (carry + cs[:, 0:LANES]))
    if mask is not None:
        a = jnp.where(mask, a, 0.0)
    return a, carry + cs[:, LANES:2 * LANES]


def _sb_attn_body(bias_ref, q_ref, k_ref, v_ref, o_ref, acc, car, *, tq, tk, qs):
    hp = pl.program_id(1)
    i = pl.program_id(2)
    jj = pl.program_id(3)
    kb = ((i + 1) * tq) // tk - 1 - jj

    @pl.when(jj == 0)
    def _():
        acc[...] = jnp.zeros_like(acc)
        car[...] = jnp.zeros_like(car)

    @pl.when(kb >= 0)
    def _():
        uo = _later_matrix()
        scale = SB_DH ** -0.5
        for e in range(2):
            bias = bias_ref[2 * hp + e]
            for sub in reversed(range(tk // LANES)):
                ks = k_ref[sub * LANES:(sub + 1) * LANES, :].astype(BF16)
                vs = v_ref[sub * LANES:(sub + 1) * LANES, :].astype(BF16)
                key0 = kb * tk + sub * LANES

                def qstep(s, _, e=e, ks=ks, vs=vs, key0=key0, bias=bias):
                    r0 = pl.multiple_of(s * qs, qs)
                    lane = _iota2((qs, LANES), 1)
                    mine = (lane >> 6) == e
                    q = jnp.where(mine, q_ref[pl.ds(r0, qs), :], 0.0).astype(BF16)
                    z = _dot_nt(q, ks) * scale + bias
                    qpos = i * tq + r0 + _iota2((qs, LANES), 0)
                    mask = (key0 + lane) < qpos
                    a, cnew = _sb_tile(z, mask, car[e, pl.ds(r0, qs), :], uo)
                    car[e, pl.ds(r0, qs), :] = cnew
                    pv = _dot(a.astype(BF16), vs)
                    acc[pl.ds(r0, qs), :] += jnp.where(mine, pv, 0.0)
                    return 0

                lax.fori_loop(0, tq // qs, qstep, 0)

    @pl.when(jj == pl.num_programs(3) - 1)
    def _():
        o_ref[...] = acc[...]


def _sb_attn(q, k, v, bias, *, bsz, t, tq, tk, qs):
    nq = t // tq
    nk = t // tk
    body = functools.partial(_sb_attn_body, tq=tq, tk=tk, qs=qs)

    def kmap(b, hp, i, jj):
        kb = jnp.maximum(((i + 1) * tq) // tk - 1 - jj, 0)
        return (b * nk + kb, hp)

    qspec = pl.BlockSpec((tq, LANES), lambda b, hp, i, jj: (b * nq + i, hp))
    kspec = pl.BlockSpec((tk, LANES), kmap)
    return pl.pallas_call(
        body, grid=(bsz, SB_HEADS // 2, nq, nk),
        in_specs=[pl.BlockSpec(memory_space=pltpu.SMEM), qspec, kspec, kspec],
        out_specs=qspec, out_shape=_sds((bsz * t, D_MODEL)),
        scratch_shapes=[pltpu.VMEM((tq, LANES), F32), pltpu.VMEM((2, tq, LANES), F32)],
        compiler_params=_cparams(("parallel", "parallel", "parallel", "arbitrary")), name="sb_attn",
    )(bias, q, k, v)


def _sb_dec_body(pt_ref, qbd_ref, bias_ref, hm_ref, kn_ref, vn_ref, kp_ref, vp_ref, o_ref,
                 acc, car, *, t_new):
    s = pl.program_id(1)
    rows = t_new * SB_HEADS
    uo = _later_matrix()

    def tile(kblk, vblk, mask):
        z = _dot_nt(qbd_ref[0].astype(BF16), kblk.astype(BF16)) * (SB_DH ** -0.5) + bias_ref[...]
        a, cnew = _sb_tile(z, mask, car[...], uo)
        car[...] = cnew
        acc[...] += _dot(a.astype(BF16), vblk.astype(BF16))

    @pl.when(s == 0)
    def _():
        acc[...] = jnp.zeros_like(acc)
        car[...] = jnp.zeros_like(car)
        lane = _iota2((rows, LANES), 1)
        tq = _iota2((rows, LANES), 0) >> 4
        tile(kn_ref[0], vn_ref[0], lane < tq)

    @pl.when(s > 0)
    def _():
        tile(kp_ref[0], vp_ref[0], None)

    @pl.when(s == pl.num_programs(1) - 1)
    def _():
        red = (acc[...] * hm_ref[...]).reshape(t_new, SB_HEADS, D_MODEL).sum(axis=1)
        o_ref[0] = jnp.concatenate([red, jnp.zeros((SUBLANES - t_new, D_MODEL), F32)], axis=0)


def _sb_decode(qbd, biasrows, headmask, k_new, v_new, cache_k, cache_v, page_table, *, t_new):
    bsz, n_pages = page_table.shape
    rows = t_new * SB_HEADS
    body = functools.partial(_sb_dec_body, t_new=t_new)

    def pmap(b, s, pt):
        return (pt[b * n_pages + (n_pages - 1) - jnp.maximum(s - 1, 0)], 0, 0)

    per_b3 = lambda shape: pl.BlockSpec(shape, lambda b, s, pt: (b, 0, 0))
    const2 = lambda a: pl.BlockSpec(a.shape, lambda b, s, pt: (0, 0))
    page = pl.BlockSpec((1, PAGE_SIZE, D_MODEL), pmap)
    grid_spec = pltpu.PrefetchScalarGridSpec(
        num_scalar_prefetch=1, grid=(bsz, n_pages + 1),
        in_specs=[per_b3((1, rows, D_MODEL)), const2(biasrows), const2(headmask),
                  per_b3((1, PAGE_SIZE, D_MODEL)), per_b3((1, PAGE_SIZE, D_MODEL)), page, page],
        out_specs=per_b3((1, SUBLANES, D_MODEL)),
        scratch_shapes=[pltpu.VMEM((rows, D_MODEL), F32), pltpu.VMEM((rows, LANES), F32)])
    return pl.pallas_call(
        body, grid_spec=grid_spec, out_shape=_sds((bsz, SUBLANES, D_MODEL)),
        compiler_params=_cparams(("parallel", "arbitrary")), name="sb_decode",
    )(page_table.reshape(-1), qbd, biasrows, headmask, k_new, v_new, cache_k, cache_v)


def _rw_pre_body(x_ref, sh0_ref, ln_ref, mu_ref, wr_ref, wk_ref, wv_ref, w1_ref, w2_ref,
                 a1_ref, a2_ref, g1_ref, g2_ref, vec_ref, bd_ref,
                 r_ref, lw_ref, km_ref, v_ref, kk_ref, b_ref, g_ref, hl_ref, hbuf,
                 *, tt, t_valid, nt):
    t = pl.program_id(1)
    h = _rms(x_ref[...], ln_ref[...])

    @pl.when(t == 0)
    def _():
        hbuf[0:SUBLANES, :] = sh0_ref[0]

    hbuf[SUBLANES:SUBLANES + tt, :] = h
    xx = hbuf[SUBLANES - 1:SUBLANES - 1 + tt, :] - h

    def mix(m):
        return (h + xx * mu_ref[m:m + 1, :]).astype(BF16)

    r_ref[...] = _dot(mix(0), wr_ref[...])
    wl = _dotb(jnp.tanh(_dot(mix(1), w1_ref[...])), w2_ref[...])
    w = -_softplus(-(vec_ref[0:1, :] + wl)) - 0.5
    k = _dot(mix(2), wk_ref[...])
    v_ref[...] = _dot(mix(3), wv_ref[...])
    a = _sigmoid(vec_ref[1:2, :] + _dotb(_dot(mix(4), a1_ref[...]), a2_ref[...]))
    g_ref[...] = _dotb(_sigmoid(_dot(mix(5), g1_ref[...])), g2_ref[...])

    row = t * tt + _iota2((tt, 1), 0)
    keep = jnp.where(row < t_valid, 1.0, 0.0)
    lw_ref[...] = -jnp.exp(w) * keep
    kt = k * vec_ref[2:3, :]
    ss = _dot_exact_rhs(kt * kt, bd_ref[...])
    kk = kt * lax.rsqrt(ss + 1e-6) * keep
    kk_ref[...] = kk
    b_ref[...] = kk * a
    km_ref[...] = k * (1.0 + (a - 1.0) * vec_ref[3:4, :]) * keep

    tl = t_valid - (nt - 1) * tt

    @pl.when(t == nt - 1)
    def _():
        hl_ref[0] = hbuf[tl:tl + SUBLANES, :]

    hbuf[0:SUBLANES, :] = hbuf[tt:tt + SUBLANES, :]


def _rw_pre(x, shift0, p, *, bsz, tp, tt, t_valid):
    nt = tp // tt
    n = bsz * tp
    row = lambda b, t: (b * nt + t, 0)
    blk = pl.BlockSpec((tt, D_MODEL), row)
    st = pl.BlockSpec((1, SUBLANES, D_MODEL), lambda b, t: (b, 0, 0))
    body = functools.partial(_rw_pre_body, tt=tt, t_valid=t_valid, nt=nt)
    consts = [p["ln"], p["mu"], p["w_r"], p["w_k"], p["w_v"], p["w1"], p["w2"], p["a1"], p["a2"],
              p["g1"], p["g2"], p["vec"], p["bd"]]
    return pl.pallas_call(
        body, grid=(bsz, nt),
        in_specs=[blk, st] + [_full(c) for c in consts],
        out_specs=[blk] * 7 + [st],
        out_shape=[_sds((n, D_MODEL))] * 7 + [_sds((bsz, SUBLANES, D_MODEL))],
        scratch_shapes=[pltpu.VMEM((tt + 2 * SUBLANES, D_MODEL), F32)],
        compiler_params=_cparams(("parallel", "arbitrary")), name="rw_pre",
    )(x, shift0, *consts)


def _rw_chunk_body(r_ref, lw_ref, km_ref, v_ref, kk_ref, b_ref, s0_ref, y_ref, sout_ref, s_scr,
                   *, c, nc, fine):
    ch = pl.program_id(1)

    @pl.when(ch == 0)
    def _():
        s_scr[...] = s0_ref[0]

    mm = _dot3 if fine else _dotb
    mm_nt = _dot3_nt if fine else _dotb_nt
    mm_tn = _dot3_tn if fine else _dotb_tn
    ri = _iota2((c, c), 0)
    ci = _iota2((c, c), 1)
    ltri = jnp.where(ci <= ri, 1.0, 0.0).astype(F32)
    lane = _iota2((c, LANES), 1)
    r2 = _iota2((LANES, LANES), 0)
    c2 = _iota2((LANES, LANES), 1)
    for p in range(RW_HEADS // 2):
        sl = slice(p * LANES, (p + 1) * LANES)
        lw = lw_ref[:, sl]
        cum = _dot(ltri, lw, HIGHEST)
        ec = jnp.exp(cum)
        em = jnp.exp(-cum)
        rt = r_ref[:, sl] * ec
        at = -kk_ref[:, sl] * jnp.exp(cum - lw)
        bvec = b_ref[:, sl]
        kmod = km_ref[:, sl]
        v = v_ref[:, sl]
        bt = bvec * em
        kt = kmod * em
        clast = cum[c - 1:c, :]
        ef = jnp.exp(clast - cum)
        s0 = s_scr[p]
        a_s = mm(at, s0)
        r_s = mm(rt, s0)
        u_parts, y_parts = [], []
        for e in range(2):
            mine = (lane >> 6) == e
            at_e = jnp.where(mine, at, 0.0)
            rt_e = jnp.where(mine, rt, 0.0)
            a_ab = jnp.where(ci < ri, _dot3_nt(at_e, bt), 0.0)
            a_ak = jnp.where(ci < ri, mm_nt(at_e, kt), 0.0)
            a_rb = jnp.where(ci <= ri, mm_nt(rt_e, bt), 0.0)
            a_rk = jnp.where(ci <= ri, mm_nt(rt_e, kt), 0.0)
            x = _tri_inv(-a_ab, c)
            u_e = _dot3(x, a_s + mm(a_ak, v))
            u_parts.append(u_e)
            y_parts.append(mm(a_rb, u_e) + mm(a_rk, v))
        mine0 = (lane >> 6) == 0
        u = jnp.where(mine0, u_parts[0], u_parts[1])
        y_ref[:, sl] = r_s + jnp.where(mine0, y_parts[0], y_parts[1])
        pcol = jnp.sum(jnp.where(r2 == c2, jnp.broadcast_to(jnp.exp(clast), (LANES, LANES)), 0.0),
                       axis=1, keepdims=True)
        upd = mm_tn(bvec * ef, u) + mm_tn(kmod * ef, v)
        s_scr[p] = s0 * pcol + jnp.where((r2 >> 6) == (c2 >> 6), upd, 0.0)

    @pl.when(ch == nc - 1)
    def _():
        sout_ref[0] = s_scr[...]


def _rw_chunk(r, lw, km, v, kk, bv, s0, *, bsz, tp, c, fine):
    nc = tp // c
    n = bsz * tp
    row = lambda b, j: (b * nc + j, 0)
    blk = pl.BlockSpec((c, D_MODEL), row)
    st = pl.BlockSpec((1, RW_HEADS // 2, LANES, LANES), lambda b, j: (b, 0, 0, 0))
    body = functools.partial(_rw_chunk_body, c=c, nc=nc, fine=fine)
    return pl.pallas_call(
        body, grid=(bsz, nc),
        in_specs=[blk] * 6 + [st],
        out_specs=[blk, st],
        out_shape=[_sds((n, D_MODEL)), _sds((bsz, RW_HEADS // 2, LANES, LANES))],
        scratch_shapes=[pltpu.VMEM((RW_HEADS // 2, LANES, LANES), F32)],
        compiler_params=_cparams(("parallel", "arbitrary")), name="rw_chunk",
    )(r, lw, km, v, kk, bv, s0)


def _rw_out_body(y_ref, r_ref, km_ref, v_ref, g_ref, x_ref, vec_ref, bd_ref, w_ref, o_ref):
    y = y_ref[...]
    inv = 1.0 / RW_HEAD
    mean = _dot_exact_rhs(y, bd_ref[...]) * inv
    d = y - mean
    var = _dot_exact_rhs(d * d, bd_ref[...]) * inv
    yn = d * lax.rsqrt(var + RW_GN_EPS) * vec_ref[4:5, :] + vec_ref[5:6, :]
    bonus = _dot_exact_rhs(r_ref[...] * km_ref[...] * vec_ref[6:7, :], bd_ref[...]) * v_ref[...]
    o_ref[...] = x_ref[...] + _dot(((yn + bonus) * g_ref[...]).astype(BF16), w_ref[...])


def _rw_out(y, r, km, v, g, x, vec, bd, w_out, tm):
    n = y.shape[0]
    blk = pl.BlockSpec((tm, D_MODEL), lambda i: (i, 0))
    return pl.pallas_call(
        _rw_out_body, grid=(n // tm,),
        in_specs=[blk] * 6 + [_full(vec), _full(bd), _full(w_out)],
        out_specs=blk, out_shape=_sds((n, D_MODEL)),
        compiler_params=_cparams(("parallel",)), name="rw_out",
    )(y, r, km, v, g, x, vec, bd, w_out)


def _rwkv_layer(x, shift0, s0, p, *, bsz, tp, tt, t_valid, chunk, fine, tm):
    r, lw, km, v, kk, bv, g, hl = _rw_pre(x, shift0, p, bsz=bsz, tp=tp, tt=tt, t_valid=t_valid)
    y, s_new = _rw_chunk(r, lw, km, v, kk, bv, s0, bsz=bsz, tp=tp, c=chunk, fine=fine)
    out = _rw_out(y, r, km, v, g, x, p["vec"], p["bd"], p["w_out"], tm)
    return out, hl[:, SUBLANES - 1, :], s_new


def _top_values(s, dst_ref, k):
    rows = s.shape[0]
    idx = _iota2(s.shape, 0)

    def step(i, s):
        m = jnp.max(s, axis=0, keepdims=True)
        dst_ref[pl.ds(i, 1), :] = m
        first = jnp.min(jnp.where(s == m, idx, rows), axis=0, keepdims=True)
        return jnp.where(idx == first, NEG_BIG, s)

    lax.fori_loop(0, k, step, s)


def _peer_route_body(x_ref, ln_ref, wqh_ref, wql_ref, skh_ref, skl_ref,
                     xn_ref, s1_ref, s2_ref, e1_ref, e2_ref, thr_ref,
                     qh_s, ql_s, top1_s, top2_s, best_s):
    xn = _rms(x_ref[...], ln_ref[...])
    xn_ref[...] = xn.astype(BF16)
    xh, xl = _split(xn)
    q_t = _dot_nt(wqh_ref[...], xh) + (_dot_nt(wqh_ref[...], xl) + _dot_nt(wql_ref[...], xh))
    qh, ql = _split(q_t)
    qh_s[...] = qh
    ql_s[...] = ql

    def head(h, _):
        for p, (s_ref, top_s) in enumerate(((s1_ref, top1_s), (s2_ref, top2_s))):
            r0 = pl.multiple_of((2 * h + p) * PEER_HALF, PEER_HALF)
            skh = skh_ref[2 * h + p]
            skl = skl_ref[2 * h + p]
            qhh = qh_s[pl.ds(r0, PEER_HALF), :]
            s = _dot(skh, qhh) + (_dot(skh, ql_s[pl.ds(r0, PEER_HALF), :]) + _dot(skl, qhh))
            s_ref[h] = s
            _top_values(s, top_s, PEER_TOPK)
        v2 = top2_s[...]
        cand = jnp.concatenate([top1_s[a:a + 1, :] + v2 for a in range(PEER_TOPK)], axis=0)
        _top_values(cand, best_s, PEER_TOPK)
        best = best_s[...]
        zsum = jnp.sum(jnp.exp(best - best[0:1, :]), axis=0, keepdims=True)
        thr_ref[pl.ds(h, 1), :] = best[PEER_TOPK - 1:PEER_TOPK, :]
        e1_ref[h] = jnp.exp(s1_ref[h] - top1_s[0:1, :]) / zsum
        e2_ref[h] = jnp.exp(s2_ref[h] - top2_s[0:1, :])
        return 0

    lax.fori_loop(0, PEER_HEADS, head, 0)


def _peer_route(x, ln, wq_hi, wq_lo, sk_hi, sk_lo, tm):
    n = x.shape[0]
    body = _peer_route_body
    sc = pl.BlockSpec((PEER_HEADS, PEER_NKEYS, tm), lambda i: (0, 0, i))
    return pl.pallas_call(
        body, grid=(n // tm,),
        scratch_shapes=[pltpu.VMEM((D_MODEL, tm), BF16), pltpu.VMEM((D_MODEL, tm), BF16),
                        pltpu.VMEM((PEER_TOPK, tm), F32), pltpu.VMEM((PEER_TOPK, tm), F32),
                        pltpu.VMEM((PEER_TOPK, tm), F32)],
        in_specs=[pl.BlockSpec((tm, D_MODEL), lambda i: (i, 0)), _full(ln), _full(wq_hi), _full(wq_lo),
                  _full(sk_hi), _full(sk_lo)],
        out_specs=[pl.BlockSpec((tm, D_MODEL), lambda i: (i, 0)), sc, sc, sc, sc,
                   pl.BlockSpec((PEER_HEADS, tm), lambda i: (0, i))],
        out_shape=[_sds((n, D_MODEL), BF16)] + [_sds((PEER_HEADS, PEER_NKEYS, n))] * 4
                  + [_sds((PEER_HEADS, n))],
        compiler_params=_cparams(("parallel",)), name="peer_route",
    )(x, ln, wq_hi, wq_lo, sk_hi, sk_lo)


def _gelu(a):
    return 0.5 * a * (1.0 + lax.erf(a * (2.0 ** -0.5)))


def _peer_main_body(xn_ref, x_ref, s1_ref, s2_ref, e1_ref, e2_ref, thr_ref, u_ref, vt_ref, o_ref,
                    act_s, p_s, acc_s, *, tm, te):
    j = pl.program_id(1)

    @pl.when(j == 0)
    def _():
        acc_s[...] = jnp.zeros_like(acc_s)

    act_s[...] = _dot_nt(u_ref[...], xn_ref[...])
    ncb = te // PEER_NKEYS
    base = pl.multiple_of(j * ncb, SUBLANES)

    def lstep(lb, _):
        ls = pl.ds(pl.multiple_of(lb * LANES, LANES), LANES)
        s1t = [s1_ref[h, pl.ds(base, ncb), ls] for h in range(PEER_HEADS)]
        e1t = [e1_ref[h, pl.ds(base, ncb), ls] for h in range(PEER_HEADS)]
        thr = thr_ref[:, ls]
        for cc in range(ncb):
            gate = jnp.zeros((PEER_NKEYS, LANES), F32)
            for h in range(PEER_HEADS):
                ssum = s1t[h][cc:cc + 1, :] + s2_ref[h, :, ls]
                sel = jnp.where(ssum >= thr[h:h + 1, :], e2_ref[h, :, ls], 0.0)
                gate = gate + sel * e1t[h][cc:cc + 1, :]
            rs = slice(cc * PEER_NKEYS, (cc + 1) * PEER_NKEYS)
            p_s[rs, ls] = (gate * _gelu(act_s[rs, ls])).astype(BF16)
        return 0

    lax.fori_loop(0, tm // LANES, lstep, 0)
    acc_s[...] += _dot(vt_ref[...], p_s[...])

    @pl.when(j == pl.num_programs(1) - 1)
    def _():
        o_ref[...] = x_ref[...] + acc_s[...].T


def _peer_main(xn, x, s1, s2, e1, e2, thr, u_bf16, vt_bf16, tm, te):
    n = x.shape[0]
    n_exp = u_bf16.shape[0]
    assert te % (SUBLANES * PEER_NKEYS) == 0 and n_exp % te == 0 and n % tm == 0
    body = functools.partial(_peer_main_body, tm=tm, te=te)
    sc = pl.BlockSpec((PEER_HEADS, PEER_NKEYS, tm), lambda i, j: (0, 0, i))
    tok = pl.BlockSpec((tm, D_MODEL), lambda i, j: (i, 0))
    return pl.pallas_call(
        body, grid=(n // tm, n_exp // te),
        in_specs=[tok, tok, sc, sc, sc, sc, pl.BlockSpec((PEER_HEADS, tm), lambda i, j: (0, i)),
                  pl.BlockSpec((te, D_MODEL), lambda i, j: (j, 0)),
                  pl.BlockSpec((D_MODEL, te), lambda i, j: (0, j))],
        out_specs=tok, out_shape=_sds((n, D_MODEL)),
        scratch_shapes=[pltpu.VMEM((te, tm), F32), pltpu.VMEM((te, tm), BF16),
                        pltpu.VMEM((D_MODEL, tm), F32)],
        compiler_params=_cparams(("parallel", "arbitrary")), name="peer_main",
    )(xn, x, s1, s2, e1, e2, thr, u_bf16, vt_bf16)


def _peer_layer(x, p, tm, te):
    xn, s1, s2, e1, e2, thr = _peer_route(x, p["ln"], p["wq_hi"], p["wq_lo"], p["sk_hi"], p["sk_lo"],
                                          min(tm, 256))
    return _peer_main(xn, x, s1, s2, e1, e2, thr, p["u"], p["vt"], tm, te)


def _row(v):
    return v.reshape(1, -1).astype(F32)


def _pad_rows(rows, total=SUBLANES):
    m = jnp.stack(rows).astype(F32)
    return jnp.pad(m, ((0, total - m.shape[0]), (0, 0)))


def _blockdiag_ones(width, block):
    i = jnp.arange(width) // block
    return (i[:, None] == i[None, :]).astype(BF16)


def _prep_dn(j, ln, dn_w_in, dn_conv_w, dn_A_log, dn_dt_bias, dn_o_norm, dn_w_out):
    pad8 = lambda v: jnp.pad(v.astype(F32), (0, LANES - DN_HEADS))
    return dict(
        ln=_row(ln),
        w_in=jnp.pad(dn_w_in[j], ((0, 0), (0, DN_PROJ_PAD - DN_PROJ))).astype(BF16),
        conv_w=_pad_rows(list(dn_conv_w[j])),
        pa=_pad_rows([pad8(dn_A_log[j]), pad8(dn_dt_bias[j])]),
        o_norm=_row(dn_o_norm[j]),
        w_out=dn_w_out[j].astype(BF16))


def _prep_rw(j, ln, rw):
    (mu, w_r, w_k, w_v, w0, w1, w2, a0, a1, a2, g1, g2, k_k, k_a, r_k, lnx_w, lnx_b, w_out) = rw
    bf = lambda w: w[j].astype(BF16)
    return dict(
        ln=_row(ln), mu=_pad_rows(list(mu[j])),
        w_r=bf(w_r), w_k=bf(w_k), w_v=bf(w_v), w1=bf(w1), w2=bf(w2), a1=bf(a1), a2=bf(a2),
        g1=bf(g1), g2=bf(g2),
        vec=_pad_rows([w0[j], a0[j], k_k[j], k_a[j], lnx_w[j], lnx_b[j], r_k[j].reshape(-1)]),
        bd=_blockdiag_ones(D_MODEL, RW_HEAD), w_out=bf(w_out))


def _prep_peer(i, ln, peer_w_q, peer_subkeys, peer_u, peer_v):
    wq_t = peer_w_q[i].T
    wq_hi = wq_t.astype(BF16)
    wq_lo = (wq_t - wq_hi.astype(F32)).astype(BF16)
    sk = peer_subkeys[i].reshape(2 * PEER_HEADS, PEER_NKEYS, PEER_HALF)
    sk_hi = sk.astype(BF16)
    sk_lo = (sk - sk_hi.astype(F32)).astype(BF16)
    return dict(ln=_row(ln), wq_hi=wq_hi, wq_lo=wq_lo, sk_hi=sk_hi, sk_lo=sk_lo,
                u=peer_u[i].astype(BF16), vt=peer_v[i].T.astype(BF16))


def _pair_states(s):
    b = s.shape[0]
    st = jnp.swapaxes(s, -1, -2).reshape(b, RW_HEADS // 2, 2, RW_HEAD, RW_HEAD)
    z = jnp.zeros_like(st[:, :, 0])
    top = jnp.concatenate([st[:, :, 0], z], axis=-1)
    bot = jnp.concatenate([z, st[:, :, 1]], axis=-1)
    return jnp.concatenate([top, bot], axis=-2)


def _unpair_states(sp):
    b = sp.shape[0]
    h0 = sp[:, :, :RW_HEAD, :RW_HEAD]
    h1 = sp[:, :, RW_HEAD:, RW_HEAD:]
    st = jnp.stack([h0, h1], axis=2).reshape(b, RW_HEADS, RW_HEAD, RW_HEAD)
    return jnp.swapaxes(st, -1, -2)


def _trunk(x, dn_state, dn_conv, rw_state, rw_shift, sb_past, params, cfg):
    bsz, tp, t_valid = cfg["bsz"], cfg["tp"], cfg["t_valid"]
    tm = cfg["tm"]
    outs = dict(dn_s=[], dn_c=[], k=[], v=[], rw_s=[], rw_x=[])
    for i in range(4):
        j = i // 3
        kind = i % 3
        if kind == 0:
            c0 = jnp.pad(dn_conv[j], ((0, 0), (SUBLANES - 3, 0), (0, 0)))
            x, tail, s_new = _delta_layer(x, c0, dn_state[j], params["dn"][j], bsz=bsz, tp=tp,
                                          tt=cfg["tt"], t_valid=t_valid, chunk=cfg["dn_chunk"],
                                          fine=cfg["fine"], tm=tm)
            outs["dn_c"].append(tail)
            outs["dn_s"].append(s_new)
        elif kind == 1:
            p = params["sb"][j]
            q, k, v = _sb_qkv(x, p["ln"], p["w_qkv"], tm)
            if sb_past is None:
                o = _sb_attn(q, k, v, p["bias"], bsz=bsz, t=tp, tq=cfg["tq"], tk=cfg["tk"], qs=cfg["qs"])
            else:
                cache_k, cache_v, page_table = sb_past
                q3 = q.reshape(bsz, tp, D_MODEL)[:, :t_valid]
                hm = p["headmask"]
                qbd = (q3[:, :, None, :] * hm[None, None]).reshape(bsz, t_valid * SB_HEADS, D_MODEL)
                padk = lambda a: jnp.pad(a.reshape(bsz, tp, D_MODEL), ((0, 0), (0, PAGE_SIZE - tp), (0, 0)))
                o = _sb_decode(qbd, p["biasrows"], jnp.tile(hm, (t_valid, 1)), padk(k), padk(v),
                               cache_k[j].reshape(-1, PAGE_SIZE, D_MODEL),
                               cache_v[j].reshape(-1, PAGE_SIZE, D_MODEL), page_table, t_new=t_valid)
                o = o.reshape(bsz * tp, D_MODEL)
            x = _mm_res(o, x, p["w_out"], tm)
            outs["k"].append(k.reshape(bsz, tp, SB_HEADS, SB_DH)[:, :t_valid])
            outs["v"].append(v.reshape(bsz, tp, SB_HEADS, SB_DH)[:, :t_valid])
        else:
            sh0 = jnp.pad(rw_shift[j][:, None, :], ((0, 0), (SUBLANES - 1, 0), (0, 0)))
            x, hl, s_new = _rwkv_layer(x, sh0, _pair_states(rw_state[j]), params["rw"][j], bsz=bsz,
                                       tp=tp, tt=cfg["tt"], t_valid=t_valid, chunk=cfg["rw_chunk"],
                                       fine=cfg["fine"], tm=tm)
            outs["rw_x"].append(hl)
            outs["rw_s"].append(_unpair_states(s_new))
        x = _peer_layer(x, params["peer"][i], cfg["peer_tm"], cfg["peer_te"])
    y = _final_norm(x, params["ln_final"], tm)
    return y, outs


def kernel(x_prompt, x_sample, state_dn, state_dn_conv, cache_k, cache_v, page_table, state_wkv, state_shift, ln_mix, ln_ffn, ln_final, dn_w_in, dn_conv_w, dn_A_log, dn_dt_bias, dn_o_norm, dn_w_out, sb_w_qkv, sb_bias, sb_w_out, rw_mu, rw_w_r, rw_w_k, rw_w_v, rw_w0, rw_w1, rw_w2, rw_a0, rw_a1, rw_a2, rw_g1, rw_g2, rw_k_k, rw_k_a, rw_r_k, rw_lnx_w, rw_lnx_b, rw_w_out, peer_w_q, peer_subkeys, peer_u, peer_v):
    bsz, seq, _ = x_prompt.shape
    dbsz, dseq, _ = x_sample.shape
    dtp = SUBLANES
    rw = (rw_mu, rw_w_r, rw_w_k, rw_w_v, rw_w0, rw_w1, rw_w2, rw_a0, rw_a1, rw_a2, rw_g1, rw_g2,
          rw_k_k, rw_k_a, rw_r_k, rw_lnx_w, rw_lnx_b, rw_w_out)
    headmask = (jnp.arange(D_MODEL)[None, :] // SB_DH == jnp.arange(SB_HEADS)[:, None]).astype(F32)
    params = dict(
        dn=[_prep_dn(j, ln_mix[3 * j], dn_w_in, dn_conv_w, dn_A_log, dn_dt_bias, dn_o_norm, dn_w_out)
            for j in range(2)],
        sb=[dict(ln=_row(ln_mix[1]), w_qkv=sb_w_qkv[0].astype(BF16), bias=sb_bias[0].astype(F32),
                 biasrows=jnp.broadcast_to(jnp.tile(sb_bias[0].astype(F32), dseq)[:, None],
                                           (dseq * SB_HEADS, LANES)),
                 headmask=headmask, w_out=sb_w_out[0].astype(BF16))],
        rw=[_prep_rw(0, ln_mix[2], rw)],
        peer=[_prep_peer(i, ln_ffn[i], peer_w_q, peer_subkeys, peer_u, peer_v) for i in range(4)],
        ln_final=_row(ln_final))

    cfg_p = dict(bsz=bsz, tp=seq, t_valid=seq, tt=256, tm=256, dn_chunk=DN_CHUNK, rw_chunk=RW_CHUNK,
                 fine=False, tq=1024, tk=512, qs=256, peer_tm=512, peer_te=1024)
    zeros = lambda *s: jnp.zeros(s, F32)
    y_p, o_p = _trunk(x_prompt.reshape(bsz * seq, D_MODEL),
                      zeros(2, bsz, DN_HEADS, DN_DK, DN_DK), zeros(2, bsz, DN_CONV - 1, 3 * DN_WIDTH),
                      zeros(1, bsz, RW_HEADS, RW_HEAD, RW_HEAD), zeros(1, bsz, D_MODEL), None,
                      params, cfg_p)

    cfg_s = dict(bsz=dbsz, tp=dtp, t_valid=dseq, tt=dtp, tm=dbsz * dtp, dn_chunk=dtp, rw_chunk=dtp,
                 fine=True, peer_tm=dbsz * dtp, peer_te=1024)
    xs = jnp.pad(x_sample, ((0, 0), (0, dtp - dseq), (0, 0))).reshape(dbsz * dtp, D_MODEL)
    y_s, o_s = _trunk(xs, state_dn, state_dn_conv, state_wkv, state_shift,
                      (cache_k, cache_v, page_table), params, cfg_s)

    def pack(y, o, b, tp, t):
        return (y.reshape(b, tp, D_MODEL)[:, :t], jnp.stack(o["dn_s"]), jnp.stack(o["dn_c"]),
                jnp.stack(o["k"]), jnp.stack(o["v"]), jnp.stack(o["rw_s"]), jnp.stack(o["rw_x"]))

    pp = pack(y_p, o_p, bsz, seq, seq)
    ps = pack(y_s, o_s, dbsz, dtp, dseq)
    return (pp[0], ps[0]) + pp[1:] + ps[1:]
```

```python
import functools
import math

import jax
import jax.numpy as jnp
from jax import lax
from jax.experimental import pallas as pl
from jax.experimental.pallas import tpu as pltpu

F32 = jnp.float32
BF16 = jnp.bfloat16
HIGHEST = lax.Precision.HIGHEST

D_MODEL = 1024
RMS_EPS = 1e-6
LANES = 128
SUBLANES = 8
MXU_DIM = 256
VMEM_LIMIT = 48 * 1024 * 1024

DN_HEADS = 8
DN_DK = 128
DN_WIDTH = 1024
DN_CONV = 4
DN_CHUNK = 64
DN_PROJ = 4 * DN_WIDTH + 2 * DN_HEADS
DN_PROJ_PAD = 4 * DN_WIDTH + LANES

SB_HEADS = 16
SB_DH = 64
PAGE_SIZE = 128

RW_HEAD = 64
RW_HEADS = 16
RW_GN_EPS = 64e-5
RW_CHUNK = 64

PEER_HEADS = 8
PEER_NKEYS = 128
PEER_HALF = 64
PEER_TOPK = 16
NEG_BIG = -3.0e38


def _dot(a, b, precision=None):
    return jnp.dot(a, b, preferred_element_type=F32, precision=precision)


def _dot_nt(a, b, precision=None):
    return lax.dot_general(a, b, (((1,), (1,)), ((), ())), preferred_element_type=F32,
                           precision=precision)


def _split(a):
    hi = a.astype(BF16)
    lo = (a - hi.astype(F32)).astype(BF16)
    return hi, lo


def _dot3(a, b):
    ah, al = _split(a)
    bh, bl = _split(b)
    return _dot(ah, bh) + (_dot(ah, bl) + _dot(al, bh))


def _dot3_nt(a, b):
    ah, al = _split(a)
    bh, bl = _split(b)
    return _dot_nt(ah, bh) + (_dot_nt(ah, bl) + _dot_nt(al, bh))


def _dot_tn(a, b):
    return lax.dot_general(a, b, (((0,), (0,)), ((), ())), preferred_element_type=F32)


def _dot3_tn(a, b):
    ah, al = _split(a)
    bh, bl = _split(b)
    return _dot_tn(ah, bh) + (_dot_tn(ah, bl) + _dot_tn(al, bh))


def _dotb_tn(a, b):
    return _dot_tn(a.astype(BF16), b.astype(BF16))


def _dotb(a, b):
    return _dot(a.astype(BF16), b.astype(BF16))


def _dotb_nt(a, b):
    return _dot_nt(a.astype(BF16), b.astype(BF16))


def _dot_exact_rhs(a, b_exact):
    ah, al = _split(a)
    return _dot(ah, b_exact) + _dot(al, b_exact)


def _rms(x, g):
    return x * lax.rsqrt(jnp.mean(x * x, axis=-1, keepdims=True) + RMS_EPS) * g


def _sigmoid(x):
    return 1.0 / (1.0 + jnp.exp(-x))


def _softplus(x):
    return jnp.maximum(x, 0.0) + jnp.log1p(jnp.exp(-jnp.abs(x)))


def _iota2(shape, axis):
    return lax.broadcasted_iota(jnp.int32, shape, axis)


def _tri_inv(m, n, c):
    ri = _iota2((n, n), 0)
    ci = _iota2((n, n), 1)
    eye = jnp.where(ri == ci, 1.0, 0.0).astype(F32)
    m8 = jnp.where((ri >> 3) == (ci >> 3), m, 0.0)
    m8_2 = _dot3(m8, m8)
    m8_4 = _dot3(m8_2, m8_2)
    x = _dot3(_dot3(eye - m8, eye + m8_2), eye + m8_4)
    size, shift = 8, 3
    while size < c:
        lowleft = jnp.where((ri >> (shift + 1)) == (ci >> (shift + 1)),
                            jnp.where(((ri >> shift) & 1) == 1,
                                      jnp.where(((ci >> shift) & 1) == 0, 1.0, 0.0), 0.0), 0.0)
        cm = m * lowleft
        x = x - _dot3(_dot3(x, cm), x)
        size, shift = size * 2, shift + 1
    return x


def _full(a):
    nd = a.ndim
    return pl.BlockSpec(a.shape, lambda *_, _nd=nd: (0,) * _nd)


def _cparams(sem):
    return pltpu.CompilerParams(dimension_semantics=sem, vmem_limit_bytes=VMEM_LIMIT)


def _sds(shape, dtype=F32):
    return jax.ShapeDtypeStruct(shape, dtype)


def _mm_res_body(a_ref, x_ref, w_ref, o_ref):
    o_ref[...] = x_ref[...] + _dot(a_ref[...].astype(BF16), w_ref[...])


def _mm_res(a, x, w_bf16, tm):
    n, k = a.shape
    m = w_bf16.shape[1]
    return pl.pallas_call(
        _mm_res_body, grid=(n // tm,),
        in_specs=[pl.BlockSpec((tm, k), lambda i: (i, 0)), pl.BlockSpec((tm, m), lambda i: (i, 0)),
                  _full(w_bf16)],
        out_specs=pl.BlockSpec((tm, m), lambda i: (i, 0)),
        out_shape=_sds((n, m)), compiler_params=_cparams(("parallel",)), name="mm_res",
    )(a, x, w_bf16)


def _final_norm_body(x_ref, g_ref, o_ref):
    o_ref[...] = _rms(x_ref[...], g_ref[...])


def _final_norm(x, g, tm):
    n = x.shape[0]
    return pl.pallas_call(
        _final_norm_body, grid=(n // tm,),
        in_specs=[pl.BlockSpec((tm, D_MODEL), lambda i: (i, 0)), _full(g)],
        out_specs=pl.BlockSpec((tm, D_MODEL), lambda i: (i, 0)),
        out_shape=_sds((n, D_MODEL)), compiler_params=_cparams(("parallel",)), name="final_norm",
    )(x, g)


def _dn_pre_body(x_ref, c0_ref, ln_ref, w_ref, cw_ref, pa_ref,
                 q_ref, k_ref, v_ref, z_ref, gc_ref, bt_ref, tail_ref,
                 proj_s, cbuf, *, tt, t_valid, nt, chunk):
    t = pl.program_id(1)
    h = _rms(x_ref[...], ln_ref[...])
    proj_s[...] = _dot(h.astype(BF16), w_ref[...])

    @pl.when(t == 0)
    def _():
        cbuf[0:SUBLANES, :] = c0_ref[0]

    cbuf[SUBLANES:SUBLANES + tt, :] = proj_s[:, 0:3 * DN_WIDTH]
    for cb in range(3 * DN_HEADS):
        sl = slice(cb * LANES, (cb + 1) * LANES)
        conv = (cbuf[5:5 + tt, sl] * cw_ref[0:1, sl] + cbuf[6:6 + tt, sl] * cw_ref[1:2, sl]
                + cbuf[7:7 + tt, sl] * cw_ref[2:3, sl] + cbuf[8:8 + tt, sl] * cw_ref[3:4, sl])
        a = conv * _sigmoid(conv)
        if cb < DN_HEADS:
            a = a * lax.rsqrt(jnp.sum(a * a, axis=-1, keepdims=True) + 1e-6) * (DN_DK ** -0.5)
            q_ref[:, sl] = a
        elif cb < 2 * DN_HEADS:
            a = a * lax.rsqrt(jnp.sum(a * a, axis=-1, keepdims=True) + 1e-6)
            k_ref[:, (cb - DN_HEADS) * LANES:(cb - DN_HEADS + 1) * LANES] = a
        else:
            v_ref[:, (cb - 2 * DN_HEADS) * LANES:(cb - 2 * DN_HEADS + 1) * LANES] = a
    z_ref[...] = proj_s[:, 3 * DN_WIDTH:4 * DN_WIDTH]

    last = proj_s[:, 4 * DN_WIDTH:DN_PROJ_PAD]
    g = -jnp.exp(pa_ref[0:1, :]) * _softplus(last + pa_ref[1:2, :])
    beta = _sigmoid(pltpu.roll(last, LANES - DN_HEADS, 1))
    row = t * tt + _iota2((tt, LANES), 0)
    lane = _iota2((tt, LANES), 1)
    keep = jnp.where(row < t_valid, jnp.where(lane < DN_HEADS, 1.0, 0.0), 0.0)
    g = g * keep
    bt_ref[...] = beta * keep
    ri = _iota2((tt, tt), 0)
    ci = _iota2((tt, tt), 1)
    sh = int(math.log2(chunk))
    ltri = jnp.where((ri >> sh) == (ci >> sh), jnp.where(ci <= ri, 1.0, 0.0), 0.0).astype(F32)
    gc_ref[...] = _dot(ltri, g, HIGHEST)

    tl = t_valid - (nt - 1) * tt

    @pl.when(t == nt - 1)
    def _():
        tail_ref[0] = cbuf[tl:tl + SUBLANES, :]

    cbuf[0:SUBLANES, :] = cbuf[tt:tt + SUBLANES, :]


def _dn_pre(x, conv0, ln, w_in, conv_w, pa, *, bsz, tp, tt, t_valid, chunk):
    nt = tp // tt
    n = bsz * tp
    row = lambda b, t: (b * nt + t, 0)
    body = functools.partial(_dn_pre_body, tt=tt, t_valid=t_valid, nt=nt, chunk=chunk)
    wide = pl.BlockSpec((tt, DN_WIDTH), row)
    narrow = pl.BlockSpec((tt, LANES), row)
    return pl.pallas_call(
        body, grid=(bsz, nt),
        in_specs=[wide, pl.BlockSpec((1, SUBLANES, 3 * DN_WIDTH), lambda b, t: (b, 0, 0)),
                  _full(ln), _full(w_in), _full(conv_w), _full(pa)],
        out_specs=[wide, wide, wide, wide, narrow, narrow,
                   pl.BlockSpec((1, SUBLANES, 3 * DN_WIDTH), lambda b, t: (b, 0, 0))],
        out_shape=[_sds((n, DN_WIDTH))] * 4 + [_sds((n, LANES))] * 2
                  + [_sds((bsz, SUBLANES, 3 * DN_WIDTH))],
        scratch_shapes=[pltpu.VMEM((tt, DN_PROJ_PAD), F32),
                        pltpu.VMEM((tt + 2 * SUBLANES, 3 * DN_WIDTH), F32)],
        compiler_params=_cparams(("parallel", "arbitrary")), name="dn_pre",
    )(x, conv0, ln, w_in, conv_w, pa)


def _dn_chunk_body(q_ref, k_ref, v_ref, gc_ref, bt_ref, s0_ref, o_ref, sout_ref, s_scr,
                   *, c, nc, fine):
    ch = pl.program_id(1)

    @pl.when(ch == 0)
    def _():
        s_scr[...] = s0_ref[0]

    mm = _dot3 if fine else _dotb
    mm_nt = _dot3_nt if fine else _dotb_nt
    mm_tn = _dot3_tn if fine else _dotb_tn
    gsz = min(DN_HEADS, MXU_DIM // c)
    rows = gsz * c
    sh = int(math.log2(c))
    ri = _iota2((rows, rows), 0)
    ci = _iota2((rows, rows), 1)
    same = (ri >> sh) == (ci >> sh)
    incl = jnp.where(same, jnp.where(ci <= ri, 1.0, 0.0), 0.0)
    strict = jnp.where(same, jnp.where(ci < ri, 1.0, 0.0), 0.0)
    for g0 in range(0, DN_HEADS, gsz):
        heads = range(g0, g0 + gsz)
        stack = lambda ref: jnp.concatenate([ref[:, h * LANES:(h + 1) * LANES] for h in heads], axis=0)
        col = lambda ref: jnp.concatenate([ref[:, h:h + 1] for h in heads], axis=0)
        q = stack(q_ref)
        k = stack(k_ref)
        v = stack(v_ref)
        gcol = col(gc_ref)
        bcol = col(bt_ref)
        glast = jnp.concatenate([jnp.broadcast_to(gc_ref[c - 1:c, h:h + 1], (c, 1)) for h in heads], axis=0)
        gmat = jnp.broadcast_to(gcol, (rows, rows))
        grow = jnp.sum(jnp.where(ri == ci, gmat, 0.0), axis=0, keepdims=True)
        decay = jnp.exp(jnp.where(incl > 0.0, gmat - grow, NEG_BIG))
        m = strict * (bcol * mm_nt(k, k) * decay)
        x = _tri_inv(m, rows, c)
        eg = jnp.exp(gcol)
        u = _dot3(x, bcol * v)
        w = _dot3(x, (bcol * eg) * k)
        qk = mm_nt(q, k) * decay
        qd = q * eg
        k_dec = k * jnp.exp(glast - gcol)
        states = [s_scr[h] for h in heads]
        blk = lambda a, i: a[i * c:(i + 1) * c, :]
        v_new = u - jnp.concatenate([mm(blk(w, i), states[i]) for i in range(gsz)], axis=0)
        o = jnp.concatenate([mm(blk(qd, i), states[i]) for i in range(gsz)], axis=0) + mm(qk, v_new)
        for i, h in enumerate(heads):
            o_ref[:, h * LANES:(h + 1) * LANES] = blk(o, i)
            s_scr[h] = (states[i] * jnp.exp(gc_ref[c - 1:c, h:h + 1])
                        + mm_tn(blk(k_dec, i), blk(v_new, i)))

    @pl.when(ch == nc - 1)
    def _():
        sout_ref[0] = s_scr[...]


def _dn_chunk(q, k, v, gc, bt, s0, *, bsz, tp, c, fine):
    nc = tp // c
    n = bsz * tp
    row = lambda b, j: (b * nc + j, 0)
    wide = pl.BlockSpec((c, DN_WIDTH), row)
    narrow = pl.BlockSpec((c, LANES), row)
    st = pl.BlockSpec((1, DN_HEADS, DN_DK, DN_DK), lambda b, j: (b, 0, 0, 0))
    body = functools.partial(_dn_chunk_body, c=c, nc=nc, fine=fine)
    return pl.pallas_call(
        body, grid=(bsz, nc),
        in_specs=[wide, wide, wide, narrow, narrow, st],
        out_specs=[wide, st],
        out_shape=[_sds((n, DN_WIDTH)), _sds((bsz, DN_HEADS, DN_DK, DN_DK))],
        scratch_shapes=[pltpu.VMEM((DN_HEADS, DN_DK, DN_DK), F32)],
        compiler_params=_cparams(("parallel", "arbitrary")), name="dn_chunk",
    )(q, k, v, gc, bt, s0)


def _dn_out_body(o_ref, z_ref, x_ref, on_ref, w_ref, y_ref):
    parts = []
    for h in range(DN_HEADS):
        sl = slice(h * LANES, (h + 1) * LANES)
        o = o_ref[:, sl]
        o = o * lax.rsqrt(jnp.mean(o * o, axis=-1, keepdims=True) + RMS_EPS) * on_ref[...]
        z = z_ref[:, sl]
        parts.append((o * (z * _sigmoid(z))).astype(BF16))
    y_ref[...] = x_ref[...] + _dot(jnp.concatenate(parts, axis=-1), w_ref[...])


def _dn_out(o, z, x, o_norm, w_out, tm):
    n = o.shape[0]
    blk = pl.BlockSpec((tm, D_MODEL), lambda i: (i, 0))
    return pl.pallas_call(
        _dn_out_body, grid=(n // tm,),
        in_specs=[blk, blk, blk, _full(o_norm), _full(w_out)],
        out_specs=blk, out_shape=_sds((n, D_MODEL)),
        compiler_params=_cparams(("parallel",)), name="dn_out",
    )(o, z, x, o_norm, w_out)


def _delta_layer(x, conv0, s0, p, *, bsz, tp, tt, t_valid, chunk, fine, tm):
    q, k, v, z, gc, bt, tail = _dn_pre(x, conv0, p["ln"], p["w_in"], p["conv_w"], p["pa"],
                                       bsz=bsz, tp=tp, tt=tt, t_valid=t_valid, chunk=chunk)
    o, s_new = _dn_chunk(q, k, v, gc, bt, s0, bsz=bsz, tp=tp, c=chunk, fine=fine)
    y = _dn_out(o, z, x, p["o_norm"], p["w_out"], tm)
    return y, tail[:, 5:8, :], s_new


def _sb_qkv_body(x_ref, ln_ref, w_ref, q_ref, k_ref, v_ref, qb_ref, kb_ref, vb_ref):
    h = _rms(x_ref[...], ln_ref[...]).astype(BF16)
    r = _dot(h, w_ref[...])
    q = r[:, 0:D_MODEL]
    k = r[:, D_MODEL:2 * D_MODEL]
    v = r[:, 2 * D_MODEL:3 * D_MODEL]
    q_ref[...] = q
    k_ref[...] = k
    v_ref[...] = v
    qb_ref[...] = (q * (SB_DH ** -0.5)).astype(BF16)
    kb_ref[...] = k.astype(BF16)
    vb_ref[...] = v.astype(BF16)


def _sb_qkv(x, ln, w, tm):
    n = x.shape[0]
    blk = pl.BlockSpec((tm, D_MODEL), lambda i: (i, 0))
    return pl.pallas_call(
        _sb_qkv_body, grid=(n // tm,),
        in_specs=[blk, _full(ln), _full(w)],
        out_specs=[blk] * 6, out_shape=[_sds((n, D_MODEL))] * 3 + [_sds((n, D_MODEL), BF16)] * 3,
        compiler_params=_cparams(("parallel",)), name="sb_qkv",
    )(x, ln, w)


def _later_matrix():
    j = _iota2((LANES, 2 * LANES), 0)
    s = _iota2((LANES, 2 * LANES), 1)
    return jnp.where(s >= LANES, 1.0, jnp.where(j > s, 1.0, 0.0)).astype(BF16)


def _sb_tile(z, mask, carry, uo):
    sp = jnp.maximum(z, 0.0) + jnp.log(1.0 + jnp.exp(-jnp.abs(z)))
    if mask is not None:
        sp = jnp.where(mask, sp, 0.0)
    cs = _dot(sp.astype(BF16), uo)
    a = jnp.exp(z - sp - (carry + cs[:, 0:LANES]))
    if mask is not None:
        a = jnp.where(mask, a, 0.0)
    return a, carry + cs[:, LANES:2 * LANES]


def _sb_attn_body(bias_ref, q_ref, k_ref, v_ref, o_ref, acc, car, *, tq, tk, qs):
    hp = pl.program_id(1)
    i = pl.program_id(2)
    jj = pl.program_id(3)
    kb = ((i + 1) * tq) // tk - 1 - jj

    @pl.when(jj == 0)
    def _():
        acc[...] = jnp.zeros_like(acc)
        car[...] = jnp.zeros_like(car)

    def run(masked):
        uo = _later_matrix()
        lane_row = _iota2((1, LANES), 1)
        lane = _iota2((tq, LANES), 1)
        q = q_ref[...]
        rel = (i * tq - kb * tk) + _iota2((tq, LANES), 0) - lane
        total = None
        for e in range(2):
            q_e = q * jnp.where((lane_row >> 6) == e, 1.0, 0.0).astype(BF16)
            carry = car[e]
            pv = jnp.zeros((tq, LANES), F32)
            for sub in reversed(range(tk // LANES)):
                ks = k_ref[sub * LANES:(sub + 1) * LANES, :]
                vs = v_ref[sub * LANES:(sub + 1) * LANES, :]
                z = _dot_nt(q_e, ks) + bias_ref[2 * hp + e]
                mask = (rel > sub * LANES) if masked else None
                a, carry = _sb_tile(z, mask, carry, uo)
                pv = pv + _dot(a.astype(BF16), vs)
            car[e] = carry
            total = pv if e == 0 else jnp.where((lane >> 6) == 0, total, pv)
        acc[...] += total

    touches_diagonal = (kb + 1) * tk > i * tq

    @pl.when(jnp.logical_and(kb >= 0, touches_diagonal))
    def _():
        run(True)

    @pl.when(jnp.logical_and(kb >= 0, jnp.logical_not(touches_diagonal)))
    def _():
        run(False)

    @pl.when(jj == pl.num_programs(3) - 1)
    def _():
        o_ref[...] = acc[...]


def _sb_attn(q, k, v, bias, *, bsz, t, tq, tk, qs):
    nq = t // tq
    nk = t // tk
    body = functools.partial(_sb_attn_body, tq=tq, tk=tk, qs=qs)

    def kmap(b, hp, i, jj):
        kb = jnp.maximum(((i + 1) * tq) // tk - 1 - jj, 0)
        return (b * nk + kb, hp)

    qspec = pl.BlockSpec((tq, LANES), lambda b, hp, i, jj: (b * nq + i, hp))
    kspec = pl.BlockSpec((tk, LANES), kmap)
    return pl.pallas_call(
        body, grid=(bsz, SB_HEADS // 2, nq, nk),
        in_specs=[pl.BlockSpec(memory_space=pltpu.SMEM), qspec, kspec, kspec],
        out_specs=qspec, out_shape=_sds((bsz * t, D_MODEL)),
        scratch_shapes=[pltpu.VMEM((tq, LANES), F32), pltpu.VMEM((2, tq, LANES), F32)],
        compiler_params=_cparams(("parallel", "parallel", "parallel", "arbitrary")), name="sb_attn",
    )(bias, q, k, v)


def _sb_dec_body(pt_ref, qbd_ref, bias_ref, hm_ref, kn_ref, vn_ref, kp_ref, vp_ref, o_ref,
                 acc, car, *, t_new):
    s = pl.program_id(1)
    rows = t_new * SB_HEADS
    uo = _later_matrix()

    def tile(kblk, vblk, mask):
        z = _dot_nt(qbd_ref[0].astype(BF16), kblk.astype(BF16)) * (SB_DH ** -0.5) + bias_ref[...]
        a, cnew = _sb_tile(z, mask, car[...], uo)
        car[...] = cnew
        acc[...] += _dot(a.astype(BF16), vblk.astype(BF16))

    @pl.when(s == 0)
    def _():
        acc[...] = jnp.zeros_like(acc)
        car[...] = jnp.zeros_like(car)
        lane = _iota2((rows, LANES), 1)
        tq = _iota2((rows, LANES), 0) >> 4
        tile(kn_ref[0], vn_ref[0], lane < tq)

    @pl.when(s > 0)
    def _():
        tile(kp_ref[0], vp_ref[0], None)

    @pl.when(s == pl.num_programs(1) - 1)
    def _():
        red = (acc[...] * hm_ref[...]).reshape(t_new, SB_HEADS, D_MODEL).sum(axis=1)
        o_ref[0] = jnp.concatenate([red, jnp.zeros((SUBLANES - t_new, D_MODEL), F32)], axis=0)


def _sb_decode(qbd, biasrows, headmask, k_new, v_new, cache_k, cache_v, page_table, *, t_new):
    bsz, n_pages = page_table.shape
    rows = t_new * SB_HEADS
    body = functools.partial(_sb_dec_body, t_new=t_new)

    def pmap(b, s, pt):
        return (pt[b * n_pages + (n_pages - 1) - jnp.maximum(s - 1, 0)], 0, 0)

    per_b3 = lambda shape: pl.BlockSpec(shape, lambda b, s, pt: (b, 0, 0))
    const2 = lambda a: pl.BlockSpec(a.shape, lambda b, s, pt: (0, 0))
    page = pl.BlockSpec((1, PAGE_SIZE, D_MODEL), pmap)
    grid_spec = pltpu.PrefetchScalarGridSpec(
        num_scalar_prefetch=1, grid=(bsz, n_pages + 1),
        in_specs=[per_b3((1, rows, D_MODEL)), const2(biasrows), const2(headmask),
                  per_b3((1, PAGE_SIZE, D_MODEL)), per_b3((1, PAGE_SIZE, D_MODEL)), page, page],
        out_specs=per_b3((1, SUBLANES, D_MODEL)),
        scratch_shapes=[pltpu.VMEM((rows, D_MODEL), F32), pltpu.VMEM((rows, LANES), F32)])
    return pl.pallas_call(
        body, grid_spec=grid_spec, out_shape=_sds((bsz, SUBLANES, D_MODEL)),
        compiler_params=_cparams(("parallel", "arbitrary")), name="sb_decode",
    )(page_table.reshape(-1), qbd, biasrows, headmask, k_new, v_new, cache_k, cache_v)


def _rw_pre_body(x_ref, sh0_ref, ln_ref, mu_ref, wr_ref, wk_ref, wv_ref, w1_ref, w2_ref,
                 a1_ref, a2_ref, g1_ref, g2_ref, vec_ref, bd_ref,
                 r_ref, lw_ref, km_ref, v_ref, kk_ref, b_ref, g_ref, hl_ref, hbuf,
                 *, tt, t_valid, nt):
    t = pl.program_id(1)
    h = _rms(x_ref[...], ln_ref[...])

    @pl.when(t == 0)
    def _():
        hbuf[0:SUBLANES, :] = sh0_ref[0]

    hbuf[SUBLANES:SUBLANES + tt, :] = h
    xx = hbuf[SUBLANES - 1:SUBLANES - 1 + tt, :] - h

    def mix(m):
        return (h + xx * mu_ref[m:m + 1, :]).astype(BF16)

    r_ref[...] = _dot(mix(0), wr_ref[...])
    wl = _dotb(jnp.tanh(_dot(mix(1), w1_ref[...])), w2_ref[...])
    w = -_softplus(-(vec_ref[0:1, :] + wl)) - 0.5
    k = _dot(mix(2), wk_ref[...])
    v_ref[...] = _dot(mix(3), wv_ref[...])
    a = _sigmoid(vec_ref[1:2, :] + _dotb(_dot(mix(4), a1_ref[...]), a2_ref[...]))
    g_ref[...] = _dotb(_sigmoid(_dot(mix(5), g1_ref[...])), g2_ref[...])

    row = t * tt + _iota2((tt, 1), 0)
    keep = jnp.where(row < t_valid, 1.0, 0.0)
    lw_ref[...] = -jnp.exp(w) * keep
    kt = k * vec_ref[2:3, :]
    ss = _dot_exact_rhs(kt * kt, bd_ref[...])
    kk = kt * lax.rsqrt(ss + 1e-6) * keep
    kk_ref[...] = kk
    b_ref[...] = kk * a
    km_ref[...] = k * (1.0 + (a - 1.0) * vec_ref[3:4, :]) * keep

    tl = t_valid - (nt - 1) * tt

    @pl.when(t == nt - 1)
    def _():
        hl_ref[0] = hbuf[tl:tl + SUBLANES, :]

    hbuf[0:SUBLANES, :] = hbuf[tt:tt + SUBLANES, :]


def _rw_pre(x, shift0, p, *, bsz, tp, tt, t_valid):
    nt = tp // tt
    n = bsz * tp
    row = lambda b, t: (b * nt + t, 0)
    blk = pl.BlockSpec((tt, D_MODEL), row)
    st = pl.BlockSpec((1, SUBLANES, D_MODEL), lambda b, t: (b, 0, 0))
    body = functools.partial(_rw_pre_body, tt=tt, t_valid=t_valid, nt=nt)
    consts = [p["ln"], p["mu"], p["w_r"], p["w_k"], p["w_v"], p["w1"], p["w2"], p["a1"], p["a2"],
              p["g1"], p["g2"], p["vec"], p["bd"]]
    return pl.pallas_call(
        body, grid=(bsz, nt),
        in_specs=[blk, st] + [_full(c) for c in consts],
        out_specs=[blk] * 7 + [st],
        out_shape=[_sds((n, D_MODEL))] * 7 + [_sds((bsz, SUBLANES, D_MODEL))],
        scratch_shapes=[pltpu.VMEM((tt + 2 * SUBLANES, D_MODEL), F32)],
        compiler_params=_cparams(("parallel", "arbitrary")), name="rw_pre",
    )(x, shift0, *consts)


def _rw_chunk_body(r_ref, lw_ref, km_ref, v_ref, kk_ref, b_ref, s0_ref, y_ref, sout_ref, s_scr,
                   *, c, nc, fine):
    ch = pl.program_id(1)

    @pl.when(ch == 0)
    def _():
        s_scr[...] = s0_ref[0]

    mm = _dot3 if fine else _dotb
    mm_nt = _dot3_nt if fine else _dotb_nt
    mm_tn = _dot3_tn if fine else _dotb_tn
    gsz = 4
    gl = gsz * RW_HEAD
    rows = gsz * c
    sh = int(math.log2(c))
    ri = _iota2((rows, rows), 0)
    ci = _iota2((rows, rows), 1)
    same = (ri >> sh) == (ci >> sh)
    incl = jnp.where(same, jnp.where(ci <= ri, 1.0, 0.0), 0.0)
    strict = jnp.where(same, jnp.where(ci < ri, 1.0, 0.0), 0.0)
    rc = _iota2((c, c), 0)
    cc = _iota2((c, c), 1)
    ltri = jnp.where(cc <= rc, 1.0, 0.0).astype(F32)
    mine = (_iota2((rows, gl), 0) >> sh) == (_iota2((rows, gl), 1) >> 6)
    lane_head = _iota2((c, gl), 1) >> 6
    r2 = _iota2((gl, gl), 0)
    c2 = _iota2((gl, gl), 1)
    head_diag = (r2 >> 6) == (c2 >> 6)
    tile = lambda a: jnp.concatenate([a] * gsz, axis=0)
    for gi in range(RW_HEADS // gsz):
        sl = slice(gi * gl, (gi + 1) * gl)
        lw = lw_ref[:, sl]
        cum = _dot(ltri, lw, HIGHEST)
        ec = jnp.exp(cum)
        em = jnp.exp(-cum)
        bvec = b_ref[:, sl]
        kmod = km_ref[:, sl]
        v = v_ref[:, sl]
        clast = cum[c - 1:c, :]
        ef = jnp.exp(clast - cum)
        rt_s = jnp.where(mine, tile(r_ref[:, sl] * ec), 0.0)
        at_s = jnp.where(mine, tile(-kk_ref[:, sl] * jnp.exp(cum - lw)), 0.0)
        bt_t = tile(bvec * em)
        kt_t = tile(kmod * em)
        v_t = tile(v)
        a_ab = strict * _dot3_nt(at_s, bt_t)
        a_ak = strict * mm_nt(at_s, kt_t)
        a_rb = incl * mm_nt(rt_s, bt_t)
        a_rk = incl * mm_nt(rt_s, kt_t)
        x = _tri_inv(-a_ab, rows, c)
        s_lo = s_scr[2 * gi]
        s_hi = s_scr[2 * gi + 1]
        times_state = lambda a: jnp.concatenate([mm(a[:, :LANES], s_lo), mm(a[:, LANES:], s_hi)], axis=1)
        u = _dot3(x, times_state(at_s) + mm(a_ak, v_t))
        y_st = times_state(rt_s) + mm(a_rb, u) + mm(a_rk, v_t)
        y = jnp.zeros((c, gl), F32)
        for g in range(gsz):
            y = jnp.where(lane_head == g, y_st[g * c:(g + 1) * c, :], y)
        y_ref[:, sl] = y
        pcol = jnp.sum(jnp.where(r2 == c2, jnp.broadcast_to(jnp.exp(clast), (gl, gl)), 0.0),
                       axis=1, keepdims=True)
        upd = mm_tn(jnp.where(mine, tile(bvec * ef), 0.0), u) + mm_tn(kmod * ef, v)
        upd = jnp.where(head_diag, upd, 0.0)
        s_scr[2 * gi] = s_lo * pcol[:LANES, :] + upd[:LANES, :LANES]
        s_scr[2 * gi + 1] = s_hi * pcol[LANES:, :] + upd[LANES:, LANES:]

    @pl.when(ch == nc - 1)
    def _():
        sout_ref[0] = s_scr[...]


def _rw_chunk(r, lw, km, v, kk, bv, s0, *, bsz, tp, c, fine):
    nc = tp // c
    n = bsz * tp
    row = lambda b, j: (b * nc + j, 0)
    blk = pl.BlockSpec((c, D_MODEL), row)
    st = pl.BlockSpec((1, RW_HEADS // 2, LANES, LANES), lambda b, j: (b, 0, 0, 0))
    body = functools.partial(_rw_chunk_body, c=c, nc=nc, fine=fine)
    return pl.pallas_call(
        body, grid=(bsz, nc),
        in_specs=[blk] * 6 + [st],
        out_specs=[blk, st],
        out_shape=[_sds((n, D_MODEL)), _sds((bsz, RW_HEADS // 2, LANES, LANES))],
        scratch_shapes=[pltpu.VMEM((RW_HEADS // 2, LANES, LANES), F32)],
        compiler_params=_cparams(("parallel", "arbitrary")), name="rw_chunk",
    )(r, lw, km, v, kk, bv, s0)


def _rw_out_body(y_ref, r_ref, km_ref, v_ref, g_ref, x_ref, vec_ref, bd_ref, w_ref, o_ref):
    y = y_ref[...]
    inv = 1.0 / RW_HEAD
    mean = _dot_exact_rhs(y, bd_ref[...]) * inv
    d = y - mean
    var = _dot_exact_rhs(d * d, bd_ref[...]) * inv
    yn = d * lax.rsqrt(var + RW_GN_EPS) * vec_ref[4:5, :] + vec_ref[5:6, :]
    bonus = _dot_exact_rhs(r_ref[...] * km_ref[...] * vec_ref[6:7, :], bd_ref[...]) * v_ref[...]
    o_ref[...] = x_ref[...] + _dot(((yn + bonus) * g_ref[...]).astype(BF16), w_ref[...])


def _rw_out(y, r, km, v, g, x, vec, bd, w_out, tm):
    n = y.shape[0]
    blk = pl.BlockSpec((tm, D_MODEL), lambda i: (i, 0))
    return pl.pallas_call(
        _rw_out_body, grid=(n // tm,),
        in_specs=[blk] * 6 + [_full(vec), _full(bd), _full(w_out)],
        out_specs=blk, out_shape=_sds((n, D_MODEL)),
        compiler_params=_cparams(("parallel",)), name="rw_out",
    )(y, r, km, v, g, x, vec, bd, w_out)


def _rwkv_layer(x, shift0, s0, p, *, bsz, tp, tt, t_valid, chunk, fine, tm):
    r, lw, km, v, kk, bv, g, hl = _rw_pre(x, shift0, p, bsz=bsz, tp=tp, tt=tt, t_valid=t_valid)
    y, s_new = _rw_chunk(r, lw, km, v, kk, bv, s0, bsz=bsz, tp=tp, c=chunk, fine=fine)
    out = _rw_out(y, r, km, v, g, x, p["vec"], p["bd"], p["w_out"], tm)
    return out, hl[:, SUBLANES - 1, :], s_new


def _top_values(arrs, dsts, k):
    idxs = [_iota2(a.shape, 0) for a in arrs]

    def step(i, carry):
        out = []
        for s, dst, idx in zip(carry, dsts, idxs):
            m = jnp.max(s, axis=0, keepdims=True)
            dst[pl.ds(i, 1), :] = m
            first = jnp.min(jnp.where(s == m, idx, s.shape[0]), axis=0, keepdims=True)
            out.append(jnp.where(idx == first, NEG_BIG, s))
        return tuple(out)

    lax.fori_loop(0, k, step, tuple(arrs))


def _pair_candidates(top1, top2):
    t2_8 = top2[0:SUBLANES, :]
    sub = _iota2(t2_8.shape, 0)
    pieces = [top1[0:1, :] + top2[...], top1[1:2, :] + t2_8]
    for a in range(2, SUBLANES):
        pieces.append(jnp.where(sub < PEER_TOPK // (a + 1), top1[a:a + 1, :] + t2_8, NEG_BIG))
    pieces.append(top1[SUBLANES:PEER_TOPK, :] + top2[0:1, :])
    return jnp.concatenate(pieces, axis=0)


def _peer_route_body(x_ref, ln_ref, wqh_ref, wql_ref, skh_ref, skl_ref,
                     xn_ref, s1_ref, s2_ref, e1_ref, e2_ref, thr_ref,
                     qh_s, ql_s, top_s, best_s):
    xn = _rms(x_ref[...], ln_ref[...])
    xn_ref[...] = xn.astype(BF16)
    xh, xl = _split(xn)
    q_t = _dot_nt(wqh_ref[...], xh) + (_dot_nt(wqh_ref[...], xl) + _dot_nt(wql_ref[...], xh))
    qh, ql = _split(q_t)
    qh_s[...] = qh
    ql_s[...] = ql

    def head_pair(hp, _):
        scores = []
        for e in range(2):
            for p, s_ref in enumerate((s1_ref, s2_ref)):
                idx = 2 * (2 * hp + e) + p
                r0 = pl.multiple_of(idx * PEER_HALF, PEER_HALF)
                skh = skh_ref[idx]
                qhh = qh_s[pl.ds(r0, PEER_HALF), :]
                s = _dot(skh, qhh) + (_dot(skh, ql_s[pl.ds(r0, PEER_HALF), :]) + _dot(skl_ref[idx], qhh))
                s_ref[2 * hp + e] = s
                scores.append(s)
        _top_values(scores, [top_s.at[i] for i in range(4)], PEER_TOPK)
        cands = [_pair_candidates(top_s.at[2 * e], top_s.at[2 * e + 1]) for e in range(2)]
        _top_values(cands, [best_s.at[e] for e in range(2)], PEER_TOPK)
        for e in range(2):
            h = 2 * hp + e
            best = best_s[e]
            zsum = jnp.sum(jnp.exp(best - best[0:1, :]), axis=0, keepdims=True)
            thr_ref[pl.ds(h, 1), :] = best[PEER_TOPK - 1:PEER_TOPK, :]
            e1_ref[h] = jnp.exp(s1_ref[h] - top_s[2 * e, 0:1, :]) / zsum
            e2_ref[h] = jnp.exp(s2_ref[h] - top_s[2 * e + 1, 0:1, :])
        return 0

    lax.fori_loop(0, PEER_HEADS // 2, head_pair, 0)


def _peer_route(x, ln, wq_hi, wq_lo, sk_hi, sk_lo, tm):
    n = x.shape[0]
    body = _peer_route_body
    sc = pl.BlockSpec((PEER_HEADS, PEER_NKEYS, tm), lambda i: (0, 0, i))
    return pl.pallas_call(
        body, grid=(n // tm,),
        scratch_shapes=[pltpu.VMEM((D_MODEL, tm), BF16), pltpu.VMEM((D_MODEL, tm), BF16),
                        pltpu.VMEM((4, PEER_TOPK, tm), F32), pltpu.VMEM((2, PEER_TOPK, tm), F32)],
        in_specs=[pl.BlockSpec((tm, D_MODEL), lambda i: (i, 0)), _full(ln), _full(wq_hi), _full(wq_lo),
                  _full(sk_hi), _full(sk_lo)],
        out_specs=[pl.BlockSpec((tm, D_MODEL), lambda i: (i, 0)), sc, sc, sc, sc,
                   pl.BlockSpec((PEER_HEADS, tm), lambda i: (0, i))],
        out_shape=[_sds((n, D_MODEL), BF16)] + [_sds((PEER_HEADS, PEER_NKEYS, n))] * 4
                  + [_sds((PEER_HEADS, n))],
        compiler_params=_cparams(("parallel",)), name="peer_route",
    )(x, ln, wq_hi, wq_lo, sk_hi, sk_lo)


def _gelu(a):
    return 0.5 * a * (1.0 + lax.erf(a * (2.0 ** -0.5)))


def _peer_main_body(xn_ref, x_ref, s1_ref, s2_ref, e1_ref, e2_ref, thr_ref, u_ref, vt_ref, o_ref,
                    act_s, p_s, acc_s, *, tm, te, tc):
    j = pl.program_id(1)

    @pl.when(j == 0)
    def _():
        acc_s[...] = jnp.zeros_like(acc_s)

    ncb = te // PEER_NKEYS
    base = pl.multiple_of(j * ncb, SUBLANES)
    chains = tm // tc
    for ch in range(chains):
        act_s[ch] = _dot_nt(u_ref[...], xn_ref[ch * tc:(ch + 1) * tc, :])
    for ch in range(chains):
        for lb in range(tc // LANES):
            ls = slice(ch * tc + lb * LANES, ch * tc + (lb + 1) * LANES)
            lc = slice(lb * LANES, (lb + 1) * LANES)
            s1t = [s1_ref[h, pl.ds(base, ncb), ls] for h in range(PEER_HEADS)]
            e1t = [e1_ref[h, pl.ds(base, ncb), ls] for h in range(PEER_HEADS)]
            thr = thr_ref[:, ls]
            for cc in range(ncb):
                gate = jnp.zeros((PEER_NKEYS, LANES), F32)
                for h in range(PEER_HEADS):
                    ssum = s1t[h][cc:cc + 1, :] + s2_ref[h, :, ls]
                    sel = jnp.where(ssum >= thr[h:h + 1, :], e2_ref[h, :, ls], 0.0)
                    gate = gate + sel * e1t[h][cc:cc + 1, :]
                rs = slice(cc * PEER_NKEYS, (cc + 1) * PEER_NKEYS)
                p_s[ch, rs, lc] = (gate * _gelu(act_s[ch, rs, lc])).astype(BF16)
        acc_s[:, ch * tc:(ch + 1) * tc] += _dot(vt_ref[...], p_s[ch])

    @pl.when(j == pl.num_programs(1) - 1)
    def _():
        o_ref[...] = x_ref[...] + acc_s[...].T


def _peer_main(xn, x, s1, s2, e1, e2, thr, u_bf16, vt_bf16, tm, te):
    n = x.shape[0]
    n_exp = u_bf16.shape[0]
    assert te % (SUBLANES * PEER_NKEYS) == 0 and n_exp % te == 0 and n % tm == 0
    tc = min(tm, MXU_DIM)
    body = functools.partial(_peer_main_body, tm=tm, te=te, tc=tc)
    sc = pl.BlockSpec((PEER_HEADS, PEER_NKEYS, tm), lambda i, j: (0, 0, i))
    tok = pl.BlockSpec((tm, D_MODEL), lambda i, j: (i, 0))
    return pl.pallas_call(
        body, grid=(n // tm, n_exp // te),
        in_specs=[tok, tok, sc, sc, sc, sc, pl.BlockSpec((PEER_HEADS, tm), lambda i, j: (0, i)),
                  pl.BlockSpec((te, D_MODEL), lambda i, j: (j, 0)),
                  pl.BlockSpec((D_MODEL, te), lambda i, j: (0, j))],
        out_specs=tok, out_shape=_sds((n, D_MODEL)),
        scratch_shapes=[pltpu.VMEM((tm // tc, te, tc), F32), pltpu.VMEM((tm // tc, te, tc), BF16),
                        pltpu.VMEM((D_MODEL, tm), F32)],
        compiler_params=_cparams(("parallel", "arbitrary")), name="peer_main",
    )(xn, x, s1, s2, e1, e2, thr, u_bf16, vt_bf16)


def _peer_layer(x, p, tm, te):
    xn, s1, s2, e1, e2, thr = _peer_route(x, p["ln"], p["wq_hi"], p["wq_lo"], p["sk_hi"], p["sk_lo"],
                                          min(tm, 256))
    return _peer_main(xn, x, s1, s2, e1, e2, thr, p["u"], p["vt"], tm, te)


def _row(v):
    return v.reshape(1, -1).astype(F32)


def _pad_rows(rows, total=SUBLANES):
    m = jnp.stack(rows).astype(F32)
    return jnp.pad(m, ((0, total - m.shape[0]), (0, 0)))


def _blockdiag_ones(width, block):
    i = jnp.arange(width) // block
    return (i[:, None] == i[None, :]).astype(BF16)


def _prep_dn(j, ln, dn_w_in, dn_conv_w, dn_A_log, dn_dt_bias, dn_o_norm, dn_w_out):
    pad8 = lambda v: jnp.pad(v.astype(F32), (0, LANES - DN_HEADS))
    return dict(
        ln=_row(ln),
        w_in=jnp.pad(dn_w_in[j], ((0, 0), (0, DN_PROJ_PAD - DN_PROJ))).astype(BF16),
        conv_w=_pad_rows(list(dn_conv_w[j])),
        pa=_pad_rows([pad8(dn_A_log[j]), pad8(dn_dt_bias[j])]),
        o_norm=_row(dn_o_norm[j]),
        w_out=dn_w_out[j].astype(BF16))


def _prep_rw(j, ln, rw):
    (mu, w_r, w_k, w_v, w0, w1, w2, a0, a1, a2, g1, g2, k_k, k_a, r_k, lnx_w, lnx_b, w_out) = rw
    bf = lambda w: w[j].astype(BF16)
    return dict(
        ln=_row(ln), mu=_pad_rows(list(mu[j])),
        w_r=bf(w_r), w_k=bf(w_k), w_v=bf(w_v), w1=bf(w1), w2=bf(w2), a1=bf(a1), a2=bf(a2),
        g1=bf(g1), g2=bf(g2),
        vec=_pad_rows([w0[j], a0[j], k_k[j], k_a[j], lnx_w[j], lnx_b[j], r_k[j].reshape(-1)]),
        bd=_blockdiag_ones(D_MODEL, RW_HEAD), w_out=bf(w_out))


def _prep_peer(i, ln, peer_w_q, peer_subkeys, peer_u, peer_v):
    wq_t = peer_w_q[i].T
    wq_hi = wq_t.astype(BF16)
    wq_lo = (wq_t - wq_hi.astype(F32)).astype(BF16)
    sk = peer_subkeys[i].reshape(2 * PEER_HEADS, PEER_NKEYS, PEER_HALF)
    sk_hi = sk.astype(BF16)
    sk_lo = (sk - sk_hi.astype(F32)).astype(BF16)
    return dict(ln=_row(ln), wq_hi=wq_hi, wq_lo=wq_lo, sk_hi=sk_hi, sk_lo=sk_lo,
                u=peer_u[i].astype(BF16), vt=peer_v[i].T.astype(BF16))


def _pair_states(s):
    b = s.shape[0]
    st = jnp.swapaxes(s, -1, -2).reshape(b, RW_HEADS // 2, 2, RW_HEAD, RW_HEAD)
    z = jnp.zeros_like(st[:, :, 0])
    top = jnp.concatenate([st[:, :, 0], z], axis=-1)
    bot = jnp.concatenate([z, st[:, :, 1]], axis=-1)
    return jnp.concatenate([top, bot], axis=-2)


def _unpair_states(sp):
    b = sp.shape[0]
    h0 = sp[:, :, :RW_HEAD, :RW_HEAD]
    h1 = sp[:, :, RW_HEAD:, RW_HEAD:]
    st = jnp.stack([h0, h1], axis=2).reshape(b, RW_HEADS, RW_HEAD, RW_HEAD)
    return jnp.swapaxes(st, -1, -2)


def _trunk(x, dn_state, dn_conv, rw_state, rw_shift, sb_past, params, cfg):
    bsz, tp, t_valid = cfg["bsz"], cfg["tp"], cfg["t_valid"]
    tm = cfg["tm"]
    outs = dict(dn_s=[], dn_c=[], k=[], v=[], rw_s=[], rw_x=[])
    for i in range(4):
        j = i // 3
        kind = i % 3
        if kind == 0:
            c0 = jnp.pad(dn_conv[j], ((0, 0), (SUBLANES - 3, 0), (0, 0)))
            x, tail, s_new = _delta_layer(x, c0, dn_state[j], params["dn"][j], bsz=bsz, tp=tp,
                                          tt=cfg["tt"], t_valid=t_valid, chunk=cfg["dn_chunk"],
                                          fine=cfg["fine"], tm=tm)
            outs["dn_c"].append(tail)
            outs["dn_s"].append(s_new)
        elif kind == 1:
            p = params["sb"][j]
            q, k, v, qb, kb, vb = _sb_qkv(x, p["ln"], p["w_qkv"], tm)
            if sb_past is None:
                o = _sb_attn(qb, kb, vb, p["bias"], bsz=bsz, t=tp, tq=cfg["tq"], tk=cfg["tk"], qs=cfg["qs"])
            else:
                cache_k, cache_v, page_table = sb_past
                q3 = q.reshape(bsz, tp, D_MODEL)[:, :t_valid]
                hm = p["headmask"]
                qbd = (q3[:, :, None, :] * hm[None, None]).reshape(bsz, t_valid * SB_HEADS, D_MODEL)
                padk = lambda a: jnp.pad(a.reshape(bsz, tp, D_MODEL), ((0, 0), (0, PAGE_SIZE - tp), (0, 0)))
                o = _sb_decode(qbd, p["biasrows"], jnp.tile(hm, (t_valid, 1)), padk(k), padk(v),
                               cache_k[j].reshape(-1, PAGE_SIZE, D_MODEL),
                               cache_v[j].reshape(-1, PAGE_SIZE, D_MODEL), page_table, t_new=t_valid)
                o = o.reshape(bsz * tp, D_MODEL)
            x = _mm_res(o, x, p["w_out"], tm)
            outs["k"].append(k.reshape(bsz, tp, SB_HEADS, SB_DH)[:, :t_valid])
            outs["v"].append(v.reshape(bsz, tp, SB_HEADS, SB_DH)[:, :t_valid])
        else:
            sh0 = jnp.pad(rw_shift[j][:, None, :], ((0, 0), (SUBLANES - 1, 0), (0, 0)))
            x, hl, s_new = _rwkv_layer(x, sh0, _pair_states(rw_state[j]), params["rw"][j], bsz=bsz,
                                       tp=tp, tt=cfg["tt"], t_valid=t_valid, chunk=cfg["rw_chunk"],
                                       fine=cfg["fine"], tm=tm)
            outs["rw_x"].append(hl)
            outs["rw_s"].append(_unpair_states(s_new))
        x = _peer_layer(x, params["peer"][i], cfg["peer_tm"], cfg["peer_te"])
    y = _final_norm(x, params["ln_final"], tm)
    return y, outs


def kernel(x_prompt, x_sample, state_dn, state_dn_conv, cache_k, cache_v, page_table, state_wkv, state_shift, ln_mix, ln_ffn, ln_final, dn_w_in, dn_conv_w, dn_A_log, dn_dt_bias, dn_o_norm, dn_w_out, sb_w_qkv, sb_bias, sb_w_out, rw_mu, rw_w_r, rw_w_k, rw_w_v, rw_w0, rw_w1, rw_w2, rw_a0, rw_a1, rw_a2, rw_g1, rw_g2, rw_k_k, rw_k_a, rw_r_k, rw_lnx_w, rw_lnx_b, rw_w_out, peer_w_q, peer_subkeys, peer_u, peer_v):
    bsz, seq, _ = x_prompt.shape
    dbsz, dseq, _ = x_sample.shape
    dtp = SUBLANES
    rw = (rw_mu, rw_w_r, rw_w_k, rw_w_v, rw_w0, rw_w1, rw_w2, rw_a0, rw_a1, rw_a2, rw_g1, rw_g2,
          rw_k_k, rw_k_a, rw_r_k, rw_lnx_w, rw_lnx_b, rw_w_out)
    headmask = (jnp.arange(D_MODEL)[None, :] // SB_DH == jnp.arange(SB_HEADS)[:, None]).astype(F32)
    params = dict(
        dn=[_prep_dn(j, ln_mix[3 * j], dn_w_in, dn_conv_w, dn_A_log, dn_dt_bias, dn_o_norm, dn_w_out)
            for j in range(2)],
        sb=[dict(ln=_row(ln_mix[1]), w_qkv=sb_w_qkv[0].astype(BF16), bias=sb_bias[0].astype(F32),
                 biasrows=jnp.broadcast_to(jnp.tile(sb_bias[0].astype(F32), dseq)[:, None],
                                           (dseq * SB_HEADS, LANES)),
                 headmask=headmask, w_out=sb_w_out[0].astype(BF16))],
        rw=[_prep_rw(0, ln_mix[2], rw)],
        peer=[_prep_peer(i, ln_ffn[i], peer_w_q, peer_subkeys, peer_u, peer_v) for i in range(4)],
        ln_final=_row(ln_final))

    cfg_p = dict(bsz=bsz, tp=seq, t_valid=seq, tt=256, tm=256, dn_chunk=DN_CHUNK, rw_chunk=RW_CHUNK,
                 fine=False, tq=512, tk=512, qs=128, peer_tm=512, peer_te=1024)
    zeros = lambda *s: jnp.zeros(s, F32)
    y_p, o_p = _trunk(x_prompt.reshape(bsz * seq, D_MODEL),
                      zeros(2, bsz, DN_HEADS, DN_DK, DN_DK), zeros(2, bsz, DN_CONV - 1, 3 * DN_WIDTH),
                      zeros(1, bsz, RW_HEADS, RW_HEAD, RW_HEAD), zeros(1, bsz, D_MODEL), None,
                      params, cfg_p)

    cfg_s = dict(bsz=dbsz, tp=dtp, t_valid=dseq, tt=dtp, tm=dbsz * dtp, dn_chunk=dtp, rw_chunk=dtp,
                 fine=True, peer_tm=dbsz * dtp, peer_te=1024)
    xs = jnp.pad(x_sample, ((0, 0), (0, dtp - dseq), (0, 0))).reshape(dbsz * dtp, D_MODEL)
    y_s, o_s = _trunk(xs, state_dn, state_dn_conv, state_wkv, state_shift,
                      (cache_k, cache_v, page_table), params, cfg_s)

    def pack(y, o, b, tp, t):
        return (y.reshape(b, tp, D_MODEL)[:, :t], jnp.stack(o["dn_s"]), jnp.stack(o["dn_c"]),
                jnp.stack(o["k"]), jnp.stack(o["v"]), jnp.stack(o["rw_s"]), jnp.stack(o["rw_x"]))

    pp = pack(y_p, o_p, bsz, seq, seq)
    ps = pack(y_s, o_s, dbsz, dtp, dseq)
    return (pp[0], ps[0]) + pp[1:] + ps[1:]
```

```python
import functools
import math

import jax
import jax.numpy as jnp
from jax import lax
from jax.experimental import pallas as pl
from jax.experimental.pallas import tpu as pltpu

F32 = jnp.float32
BF16 = jnp.bfloat16
HIGHEST = lax.Precision.HIGHEST

D_MODEL = 1024
RMS_EPS = 1e-6
LANES = 128
SUBLANES = 8
MXU_DIM = 256
VMEM_LIMIT = 48 * 1024 * 1024

DN_HEADS = 8
DN_DK = 128
DN_WIDTH = 1024
DN_CONV = 4
DN_CHUNK = 64
DN_PROJ = 4 * DN_WIDTH + 2 * DN_HEADS
DN_PROJ_PAD = 4 * DN_WIDTH + LANES

SB_HEADS = 16
SB_DH = 64
PAGE_SIZE = 128

RW_HEAD = 64
RW_HEADS = 16
RW_GN_EPS = 64e-5
RW_CHUNK = 64

PEER_HEADS = 8
PEER_NKEYS = 128
PEER_HALF = 64
PEER_TOPK = 16
NEG_BIG = -3.0e38


def _dot(a, b, precision=None):
    return jnp.dot(a, b, preferred_element_type=F32, precision=precision)


def _dot_nt(a, b, precision=None):
    return lax.dot_general(a, b, (((1,), (1,)), ((), ())), preferred_element_type=F32,
                           precision=precision)


def _split(a):
    hi = a.astype(BF16)
    lo = (a - hi.astype(F32)).astype(BF16)
    return hi, lo


def _dot3(a, b):
    ah, al = _split(a)
    bh, bl = _split(b)
    return _dot(ah, bh) + (_dot(ah, bl) + _dot(al, bh))


def _dot3_nt(a, b):
    ah, al = _split(a)
    bh, bl = _split(b)
    return _dot_nt(ah, bh) + (_dot_nt(ah, bl) + _dot_nt(al, bh))


def _dot_tn(a, b):
    return lax.dot_general(a, b, (((0,), (0,)), ((), ())), preferred_element_type=F32)


def _dot3_tn(a, b):
    ah, al = _split(a)
    bh, bl = _split(b)
    return _dot_tn(ah, bh) + (_dot_tn(ah, bl) + _dot_tn(al, bh))


def _dotb_tn(a, b):
    return _dot_tn(a.astype(BF16), b.astype(BF16))


def _dotb(a, b):
    return _dot(a.astype(BF16), b.astype(BF16))


def _dotb_nt(a, b):
    return _dot_nt(a.astype(BF16), b.astype(BF16))


def _dot_exact_rhs(a, b_exact):
    ah, al = _split(a)
    return _dot(ah, b_exact) + _dot(al, b_exact)


def _rms(x, g):
    return x * lax.rsqrt(jnp.mean(x * x, axis=-1, keepdims=True) + RMS_EPS) * g


def _sigmoid(x):
    return 1.0 / (1.0 + jnp.exp(-x))


def _softplus(x):
    return jnp.maximum(x, 0.0) + jnp.log1p(jnp.exp(-jnp.abs(x)))


def _iota2(shape, axis):
    return lax.broadcasted_iota(jnp.int32, shape, axis)


def _tri_inv(m, n, c, mm):
    ri = _iota2((n, n), 0)
    ci = _iota2((n, n), 1)
    m8 = jnp.where((ri >> 3) == (ci >> 3), m, 0.0)
    m8_2 = mm(m8, m8)
    m8_4 = mm(m8_2, m8_2)
    d = (m8_2 - m8) - mm(m8, m8_2)
    d = d + m8_4 + mm(d, m8_4)
    size, shift = 8, 3
    while size < c:
        lowleft = jnp.where((ri >> (shift + 1)) == (ci >> (shift + 1)),
                            jnp.where(((ri >> shift) & 1) == 1,
                                      jnp.where(((ci >> shift) & 1) == 0, 1.0, 0.0), 0.0), 0.0)
        cm = m * lowleft
        t = cm + mm(d, cm)
        d = d - (t + mm(t, d))
        size, shift = size * 2, shift + 1
    return d


def _full(a):
    nd = a.ndim
    return pl.BlockSpec(a.shape, lambda *_, _nd=nd: (0,) * _nd)


def _cparams(sem):
    return pltpu.CompilerParams(dimension_semantics=sem, vmem_limit_bytes=VMEM_LIMIT)


def _sds(shape, dtype=F32):
    return jax.ShapeDtypeStruct(shape, dtype)


def _mm_res_body(a_ref, x_ref, w_ref, o_ref):
    o_ref[...] = x_ref[...] + _dot(a_ref[...].astype(BF16), w_ref[...])


def _mm_res(a, x, w_bf16, tm):
    n, k = a.shape
    m = w_bf16.shape[1]
    return pl.pallas_call(
        _mm_res_body, grid=(n // tm,),
        in_specs=[pl.BlockSpec((tm, k), lambda i: (i, 0)), pl.BlockSpec((tm, m), lambda i: (i, 0)),
                  _full(w_bf16)],
        out_specs=pl.BlockSpec((tm, m), lambda i: (i, 0)),
        out_shape=_sds((n, m)), compiler_params=_cparams(("parallel",)), name="mm_res",
    )(a, x, w_bf16)


def _final_norm_body(x_ref, g_ref, o_ref):
    o_ref[...] = _rms(x_ref[...], g_ref[...])


def _final_norm(x, g, tm):
    n = x.shape[0]
    return pl.pallas_call(
        _final_norm_body, grid=(n // tm,),
        in_specs=[pl.BlockSpec((tm, D_MODEL), lambda i: (i, 0)), _full(g)],
        out_specs=pl.BlockSpec((tm, D_MODEL), lambda i: (i, 0)),
        out_shape=_sds((n, D_MODEL)), compiler_params=_cparams(("parallel",)), name="final_norm",
    )(x, g)


def _dn_pre_body(x_ref, c0_ref, ln_ref, w_ref, cw_ref, pa_ref,
                 q_ref, k_ref, v_ref, z_ref, gc_ref, bt_ref, tail_ref,
                 proj_s, cbuf, *, tt, t_valid, nt, chunk):
    t = pl.program_id(1)
    h = _rms(x_ref[...], ln_ref[...])
    proj_s[...] = _dot(h.astype(BF16), w_ref[...])

    @pl.when(t == 0)
    def _():
        cbuf[0:SUBLANES, :] = c0_ref[0]

    cbuf[SUBLANES:SUBLANES + tt, :] = proj_s[:, 0:3 * DN_WIDTH]
    for cb in range(3 * DN_HEADS):
        sl = slice(cb * LANES, (cb + 1) * LANES)
        conv = (cbuf[5:5 + tt, sl] * cw_ref[0:1, sl] + cbuf[6:6 + tt, sl] * cw_ref[1:2, sl]
                + cbuf[7:7 + tt, sl] * cw_ref[2:3, sl] + cbuf[8:8 + tt, sl] * cw_ref[3:4, sl])
        a = conv * _sigmoid(conv)
        if cb < DN_HEADS:
            a = a * lax.rsqrt(jnp.sum(a * a, axis=-1, keepdims=True) + 1e-6) * (DN_DK ** -0.5)
            q_ref[:, sl] = a
        elif cb < 2 * DN_HEADS:
            a = a * lax.rsqrt(jnp.sum(a * a, axis=-1, keepdims=True) + 1e-6)
            k_ref[:, (cb - DN_HEADS) * LANES:(cb - DN_HEADS + 1) * LANES] = a
        else:
            v_ref[:, (cb - 2 * DN_HEADS) * LANES:(cb - 2 * DN_HEADS + 1) * LANES] = a
    z_ref[...] = proj_s[:, 3 * DN_WIDTH:4 * DN_WIDTH]

    last = proj_s[:, 4 * DN_WIDTH:DN_PROJ_PAD]
    g = -jnp.exp(pa_ref[0:1, :]) * _softplus(last + pa_ref[1:2, :])
    beta = _sigmoid(pltpu.roll(last, LANES - DN_HEADS, 1))
    row = t * tt + _iota2((tt, LANES), 0)
    lane = _iota2((tt, LANES), 1)
    keep = jnp.where(row < t_valid, jnp.where(lane < DN_HEADS, 1.0, 0.0), 0.0)
    g = g * keep
    bt_ref[...] = beta * keep
    ri = _iota2((tt, tt), 0)
    ci = _iota2((tt, tt), 1)
    sh = int(math.log2(chunk))
    ltri = jnp.where((ri >> sh) == (ci >> sh), jnp.where(ci <= ri, 1.0, 0.0), 0.0).astype(F32)
    gc_ref[...] = _dot(ltri, g, HIGHEST)

    tl = t_valid - (nt - 1) * tt

    @pl.when(t == nt - 1)
    def _():
        tail_ref[0] = cbuf[tl:tl + SUBLANES, :]

    cbuf[0:SUBLANES, :] = cbuf[tt:tt + SUBLANES, :]


def _dn_pre(x, conv0, ln, w_in, conv_w, pa, *, bsz, tp, tt, t_valid, chunk):
    nt = tp // tt
    n = bsz * tp
    row = lambda b, t: (b * nt + t, 0)
    body = functools.partial(_dn_pre_body, tt=tt, t_valid=t_valid, nt=nt, chunk=chunk)
    wide = pl.BlockSpec((tt, DN_WIDTH), row)
    narrow = pl.BlockSpec((tt, LANES), row)
    return pl.pallas_call(
        body, grid=(bsz, nt),
        in_specs=[wide, pl.BlockSpec((1, SUBLANES, 3 * DN_WIDTH), lambda b, t: (b, 0, 0)),
                  _full(ln), _full(w_in), _full(conv_w), _full(pa)],
        out_specs=[wide, wide, wide, wide, narrow, narrow,
                   pl.BlockSpec((1, SUBLANES, 3 * DN_WIDTH), lambda b, t: (b, 0, 0))],
        out_shape=[_sds((n, DN_WIDTH))] * 4 + [_sds((n, LANES))] * 2
                  + [_sds((bsz, SUBLANES, 3 * DN_WIDTH))],
        scratch_shapes=[pltpu.VMEM((tt, DN_PROJ_PAD), F32),
                        pltpu.VMEM((tt + 2 * SUBLANES, 3 * DN_WIDTH), F32)],
        compiler_params=_cparams(("parallel", "arbitrary")), name="dn_pre",
    )(x, conv0, ln, w_in, conv_w, pa)


def _dn_chunk_body(q_ref, k_ref, v_ref, gc_ref, bt_ref, s0_ref, o_ref, sout_ref, s_scr,
                   *, c, nc, fine):
    ch = pl.program_id(1)

    @pl.when(ch == 0)
    def _():
        s_scr[...] = s0_ref[0]

    mm = _dot3 if fine else _dotb
    mm_nt = _dot3_nt if fine else _dotb_nt
    mm_tn = _dot3_tn if fine else _dotb_tn
    gsz = min(DN_HEADS, MXU_DIM // c)
    rows = gsz * c
    sh = int(math.log2(c))
    ri = _iota2((rows, rows), 0)
    ci = _iota2((rows, rows), 1)
    same = (ri >> sh) == (ci >> sh)
    incl = jnp.where(same, jnp.where(ci <= ri, 1.0, 0.0), 0.0)
    strict = jnp.where(same, jnp.where(ci < ri, 1.0, 0.0), 0.0)
    for g0 in range(0, DN_HEADS, gsz):
        heads = range(g0, g0 + gsz)
        stack = lambda ref: jnp.concatenate([ref[:, h * LANES:(h + 1) * LANES] for h in heads], axis=0)
        col = lambda ref: jnp.concatenate([ref[:, h:h + 1] for h in heads], axis=0)
        q = stack(q_ref)
        k = stack(k_ref)
        v = stack(v_ref)
        gcol = col(gc_ref)
        bcol = col(bt_ref)
        glast = jnp.concatenate([jnp.broadcast_to(gc_ref[c - 1:c, h:h + 1], (c, 1)) for h in heads], axis=0)
        gmat = jnp.broadcast_to(gcol, (rows, rows))
        grow = jnp.sum(jnp.where(ri == ci, gmat, 0.0), axis=0, keepdims=True)
        decay = jnp.exp(jnp.where(incl > 0.0, gmat - grow, NEG_BIG))
        m = strict * (bcol * mm_nt(k, k) * decay)
        xd = _tri_inv(m, rows, c, mm)
        eg = jnp.exp(gcol)
        rhs_u = bcol * v
        rhs_w = (bcol * eg) * k
        u = rhs_u + mm(xd, rhs_u)
        w = rhs_w + mm(xd, rhs_w)
        qk = mm_nt(q, k) * decay
        qd = q * eg
        k_dec = k * jnp.exp(glast - gcol)
        states = [s_scr[h] for h in heads]
        blk = lambda a, i: a[i * c:(i + 1) * c, :]
        v_new = u - jnp.concatenate([mm(blk(w, i), states[i]) for i in range(gsz)], axis=0)
        o = jnp.concatenate([mm(blk(qd, i), states[i]) for i in range(gsz)], axis=0) + mm(qk, v_new)
        for i, h in enumerate(heads):
            o_ref[:, h * LANES:(h + 1) * LANES] = blk(o, i)
            s_scr[h] = (states[i] * jnp.exp(gc_ref[c - 1:c, h:h + 1])
                        + mm_tn(blk(k_dec, i), blk(v_new, i)))

    @pl.when(ch == nc - 1)
    def _():
        sout_ref[0] = s_scr[...]


def _dn_chunk(q, k, v, gc, bt, s0, *, bsz, tp, c, fine):
    nc = tp // c
    n = bsz * tp
    row = lambda b, j: (b * nc + j, 0)
    wide = pl.BlockSpec((c, DN_WIDTH), row)
    narrow = pl.BlockSpec((c, LANES), row)
    st = pl.BlockSpec((1, DN_HEADS, DN_DK, DN_DK), lambda b, j: (b, 0, 0, 0))
    body = functools.partial(_dn_chunk_body, c=c, nc=nc, fine=fine)
    return pl.pallas_call(
        body, grid=(bsz, nc),
        in_specs=[wide, wide, wide, narrow, narrow, st],
        out_specs=[wide, st],
        out_shape=[_sds((n, DN_WIDTH)), _sds((bsz, DN_HEADS, DN_DK, DN_DK))],
        scratch_shapes=[pltpu.VMEM((DN_HEADS, DN_DK, DN_DK), F32)],
        compiler_params=_cparams(("parallel", "arbitrary")), name="dn_chunk",
    )(q, k, v, gc, bt, s0)


def _dn_out_body(o_ref, z_ref, x_ref, on_ref, w_ref, y_ref):
    parts = []
    for h in range(DN_HEADS):
        sl = slice(h * LANES, (h + 1) * LANES)
        o = o_ref[:, sl]
        o = o * lax.rsqrt(jnp.mean(o * o, axis=-1, keepdims=True) + RMS_EPS) * on_ref[...]
        z = z_ref[:, sl]
        parts.append((o * (z * _sigmoid(z))).astype(BF16))
    y_ref[...] = x_ref[...] + _dot(jnp.concatenate(parts, axis=-1), w_ref[...])


def _dn_out(o, z, x, o_norm, w_out, tm):
    n = o.shape[0]
    blk = pl.BlockSpec((tm, D_MODEL), lambda i: (i, 0))
    return pl.pallas_call(
        _dn_out_body, grid=(n // tm,),
        in_specs=[blk, blk, blk, _full(o_norm), _full(w_out)],
        out_specs=blk, out_shape=_sds((n, D_MODEL)),
        compiler_params=_cparams(("parallel",)), name="dn_out",
    )(o, z, x, o_norm, w_out)


def _delta_layer(x, conv0, s0, p, *, bsz, tp, tt, t_valid, chunk, fine, tm):
    q, k, v, z, gc, bt, tail = _dn_pre(x, conv0, p["ln"], p["w_in"], p["conv_w"], p["pa"],
                                       bsz=bsz, tp=tp, tt=tt, t_valid=t_valid, chunk=chunk)
    o, s_new = _dn_chunk(q, k, v, gc, bt, s0, bsz=bsz, tp=tp, c=chunk, fine=fine)
    y = _dn_out(o, z, x, p["o_norm"], p["w_out"], tm)
    return y, tail[:, 5:8, :], s_new


def _sb_qkv_body(x_ref, ln_ref, w_ref, q_ref, k_ref, v_ref, qb_ref, kb_ref, vb_ref):
    h = _rms(x_ref[...], ln_ref[...]).astype(BF16)
    r = _dot(h, w_ref[...])
    q = r[:, 0:D_MODEL]
    k = r[:, D_MODEL:2 * D_MODEL]
    v = r[:, 2 * D_MODEL:3 * D_MODEL]
    q_ref[...] = q
    k_ref[...] = k
    v_ref[...] = v
    qb_ref[...] = (q * (SB_DH ** -0.5)).astype(BF16)
    kb_ref[...] = k.astype(BF16)
    vb_ref[...] = v.astype(BF16)


def _sb_qkv(x, ln, w, tm):
    n = x.shape[0]
    blk = pl.BlockSpec((tm, D_MODEL), lambda i: (i, 0))
    return pl.pallas_call(
        _sb_qkv_body, grid=(n // tm,),
        in_specs=[blk, _full(ln), _full(w)],
        out_specs=[blk] * 6, out_shape=[_sds((n, D_MODEL))] * 3 + [_sds((n, D_MODEL), BF16)] * 3,
        compiler_params=_cparams(("parallel",)), name="sb_qkv",
    )(x, ln, w)


def _later_matrix():
    j = _iota2((LANES, 2 * LANES), 0)
    s = _iota2((LANES, 2 * LANES), 1)
    return jnp.where(s >= LANES, 1.0, jnp.where(j > s, 1.0, 0.0)).astype(BF16)


def _sb_tile(z, mask, carry, uo):
    sp = jnp.maximum(z, 0.0) + jnp.log(1.0 + jnp.exp(-jnp.abs(z)))
    if mask is not None:
        sp = jnp.where(mask, sp, 0.0)
    cs = _dot(sp.astype(BF16), uo)
    a = jnp.exp(z - sp - (carry + cs[:, 0:LANES]))
    if mask is not None:
        a = jnp.where(mask, a, 0.0)
    return a, carry + cs[:, LANES:2 * LANES]


def _sb_attn_body(bias_ref, q_ref, k_ref, v_ref, o_ref, acc, car, *, tq, tk, qs):
    hp = pl.program_id(1)
    i = pl.program_id(2)
    jj = pl.program_id(3)
    kb = ((i + 1) * tq) // tk - 1 - jj

    @pl.when(jj == 0)
    def _():
        acc[...] = jnp.zeros_like(acc)
        car[...] = jnp.zeros_like(car)

    def run(masked):
        uo = _later_matrix()
        lane_row = _iota2((1, LANES), 1)
        lane = _iota2((tq, LANES), 1)
        q = q_ref[...]
        rel = (i * tq - kb * tk) + _iota2((tq, LANES), 0) - lane
        total = None
        for e in range(2):
            q_e = q * jnp.where((lane_row >> 6) == e, 1.0, 0.0).astype(BF16)
            carry = car[e]
            pv = jnp.zeros((tq, LANES), F32)
            for sub in reversed(range(tk // LANES)):
                ks = k_ref[sub * LANES:(sub + 1) * LANES, :]
                vs = v_ref[sub * LANES:(sub + 1) * LANES, :]
                z = _dot_nt(q_e, ks) + bias_ref[2 * hp + e]
                mask = (rel > sub * LANES) if masked else None
                a, carry = _sb_tile(z, mask, carry, uo)
                pv = pv + _dot(a.astype(BF16), vs)
            car[e] = carry
            total = pv if e == 0 else jnp.where((lane >> 6) == 0, total, pv)
        acc[...] += total

    touches_diagonal = (kb + 1) * tk > i * tq

    @pl.when(jnp.logical_and(kb >= 0, touches_diagonal))
    def _():
        run(True)

    @pl.when(jnp.logical_and(kb >= 0, jnp.logical_not(touches_diagonal)))
    def _():
        run(False)

    @pl.when(jj == pl.num_programs(3) - 1)
    def _():
        o_ref[...] = acc[...]


def _sb_attn(q, k, v, bias, *, bsz, t, tq, tk, qs):
    nq = t // tq
    nk = t // tk
    body = functools.partial(_sb_attn_body, tq=tq, tk=tk, qs=qs)

    def kmap(b, hp, i, jj):
        kb = jnp.maximum(((i + 1) * tq) // tk - 1 - jj, 0)
        return (b * nk + kb, hp)

    qspec = pl.BlockSpec((tq, LANES), lambda b, hp, i, jj: (b * nq + i, hp))
    kspec = pl.BlockSpec((tk, LANES), kmap)
    return pl.pallas_call(
        body, grid=(bsz, SB_HEADS // 2, nq, nk),
        in_specs=[pl.BlockSpec(memory_space=pltpu.SMEM), qspec, kspec, kspec],
        out_specs=qspec, out_shape=_sds((bsz * t, D_MODEL)),
        scratch_shapes=[pltpu.VMEM((tq, LANES), F32), pltpu.VMEM((2, tq, LANES), F32)],
        compiler_params=_cparams(("parallel", "parallel", "parallel", "arbitrary")), name="sb_attn",
    )(bias, q, k, v)


def _sb_dec_body(pt_ref, qbd_ref, bias_ref, hm_ref, kn_ref, vn_ref, *refs, t_new, ppb):
    kp_refs, vp_refs = refs[:ppb], refs[ppb:2 * ppb]
    o_ref, acc, car, kcat, vcat = refs[2 * ppb:]
    s = pl.program_id(1)
    rows = t_new * SB_HEADS
    uo = _later_matrix()
    qb = qbd_ref[0].astype(BF16)

    def tile(kblk, vblk, mask, carry):
        z = _dot_nt(qb, kblk) * (SB_DH ** -0.5) + bias_ref[...]
        a, cnew = _sb_tile(z, mask, carry, uo)
        return _dot(a.astype(BF16), vblk), cnew

    @pl.when(s == 0)
    def _():
        lane = _iota2((rows, LANES), 1)
        tq = _iota2((rows, LANES), 0) >> 4
        pv, cnew = tile(kn_ref[0].astype(BF16), vn_ref[0].astype(BF16), lane < tq,
                        jnp.zeros((rows, LANES), F32))
        acc[...] = pv
        car[...] = cnew

    @pl.when(s > 0)
    def _():
        for r in range(ppb):
            for hp in range(SB_HEADS // 2):
                for src, dst in ((kp_refs[r], kcat), (vp_refs[r], vcat)):
                    pair = [src[pl.ds(2 * hp + e, PAGE_SIZE, stride=SB_HEADS), :] for e in range(2)]
                    dst[r, :, hp * LANES:(hp + 1) * LANES] = jnp.concatenate(pair, axis=1).astype(BF16)
        carry = car[...]
        total = acc[...]
        for r in range(ppb):
            pv, carry = tile(kcat[r], vcat[r], None, carry)
            total = total + pv
        car[...] = carry
        acc[...] = total

    @pl.when(s == pl.num_programs(1) - 1)
    def _():
        red = (acc[...] * hm_ref[...]).reshape(t_new, SB_HEADS, D_MODEL).sum(axis=1)
        o_ref[0] = jnp.concatenate([red, jnp.zeros((SUBLANES - t_new, D_MODEL), F32)], axis=0)


def _sb_decode(qbd, biasrows, headmask, k_new, v_new, cache_k, cache_v, page_table, *, t_new, ppb):
    bsz, n_pages = page_table.shape
    assert n_pages % ppb == 0
    rows = t_new * SB_HEADS
    page_rows = PAGE_SIZE * SB_HEADS
    body = functools.partial(_sb_dec_body, t_new=t_new, ppb=ppb)

    def pmap(r):
        def index(b, s, pt):
            return (pt[b * n_pages + (n_pages - 1) - (jnp.maximum(s - 1, 0) * ppb + r)], 0)
        return index

    per_b3 = lambda shape: pl.BlockSpec(shape, lambda b, s, pt: (b, 0, 0))
    const2 = lambda a: pl.BlockSpec(a.shape, lambda b, s, pt: (0, 0))
    pages = [pl.BlockSpec((page_rows, SB_DH), pmap(r)) for r in range(ppb)]
    grid_spec = pltpu.PrefetchScalarGridSpec(
        num_scalar_prefetch=1, grid=(bsz, n_pages // ppb + 1),
        in_specs=[per_b3((1, rows, D_MODEL)), const2(biasrows), const2(headmask),
                  per_b3((1, PAGE_SIZE, D_MODEL)), per_b3((1, PAGE_SIZE, D_MODEL))] + pages + pages,
        out_specs=per_b3((1, SUBLANES, D_MODEL)),
        scratch_shapes=[pltpu.VMEM((rows, D_MODEL), F32), pltpu.VMEM((rows, LANES), F32),
                        pltpu.VMEM((ppb, PAGE_SIZE, D_MODEL), BF16),
                        pltpu.VMEM((ppb, PAGE_SIZE, D_MODEL), BF16)])
    return pl.pallas_call(
        body, grid_spec=grid_spec, out_shape=_sds((bsz, SUBLANES, D_MODEL)),
        compiler_params=_cparams(("parallel", "arbitrary")), name="sb_decode",
    )(page_table.reshape(-1), qbd, biasrows, headmask, k_new, v_new,
      *([cache_k] * ppb), *([cache_v] * ppb))


def _rw_pre_body(x_ref, sh0_ref, ln_ref, mu_ref, wr_ref, wk_ref, wv_ref, w1_ref, w2_ref,
                 a1_ref, a2_ref, g1_ref, g2_ref, vec_ref, bd_ref,
                 r_ref, lw_ref, km_ref, v_ref, kk_ref, b_ref, g_ref, hl_ref, hbuf,
                 *, tt, t_valid, nt):
    t = pl.program_id(1)
    h = _rms(x_ref[...], ln_ref[...])

    @pl.when(t == 0)
    def _():
        hbuf[0:SUBLANES, :] = sh0_ref[0]

    hbuf[SUBLANES:SUBLANES + tt, :] = h
    xx = hbuf[SUBLANES - 1:SUBLANES - 1 + tt, :] - h

    def mix(m):
        return (h + xx * mu_ref[m:m + 1, :]).astype(BF16)

    r_ref[...] = _dot(mix(0), wr_ref[...])
    wl = _dotb(jnp.tanh(_dot(mix(1), w1_ref[...])), w2_ref[...])
    w = -_softplus(-(vec_ref[0:1, :] + wl)) - 0.5
    k = _dot(mix(2), wk_ref[...])
    v_ref[...] = _dot(mix(3), wv_ref[...])
    a = _sigmoid(vec_ref[1:2, :] + _dotb(_dot(mix(4), a1_ref[...]), a2_ref[...]))
    g_ref[...] = _dotb(_sigmoid(_dot(mix(5), g1_ref[...])), g2_ref[...])

    row = t * tt + _iota2((tt, 1), 0)
    keep = jnp.where(row < t_valid, 1.0, 0.0)
    lw_ref[...] = -jnp.exp(w) * keep
    kt = k * vec_ref[2:3, :]
    ss = _dot_exact_rhs(kt * kt, bd_ref[...])
    kk = kt * lax.rsqrt(ss + 1e-6) * keep
    kk_ref[...] = kk
    b_ref[...] = kk * a
    km_ref[...] = k * (1.0 + (a - 1.0) * vec_ref[3:4, :]) * keep

    tl = t_valid - (nt - 1) * tt

    @pl.when(t == nt - 1)
    def _():
        hl_ref[0] = hbuf[tl:tl + SUBLANES, :]

    hbuf[0:SUBLANES, :] = hbuf[tt:tt + SUBLANES, :]


def _rw_pre(x, shift0, p, *, bsz, tp, tt, t_valid):
    nt = tp // tt
    n = bsz * tp
    row = lambda b, t: (b * nt + t, 0)
    blk = pl.BlockSpec((tt, D_MODEL), row)
    st = pl.BlockSpec((1, SUBLANES, D_MODEL), lambda b, t: (b, 0, 0))
    body = functools.partial(_rw_pre_body, tt=tt, t_valid=t_valid, nt=nt)
    consts = [p["ln"], p["mu"], p["w_r"], p["w_k"], p["w_v"], p["w1"], p["w2"], p["a1"], p["a2"],
              p["g1"], p["g2"], p["vec"], p["bd"]]
    return pl.pallas_call(
        body, grid=(bsz, nt),
        in_specs=[blk, st] + [_full(c) for c in consts],
        out_specs=[blk] * 7 + [st],
        out_shape=[_sds((n, D_MODEL))] * 7 + [_sds((bsz, SUBLANES, D_MODEL))],
        scratch_shapes=[pltpu.VMEM((tt + 2 * SUBLANES, D_MODEL), F32)],
        compiler_params=_cparams(("parallel", "arbitrary")), name="rw_pre",
    )(x, shift0, *consts)


def _rw_chunk_body(r_ref, lw_ref, km_ref, v_ref, kk_ref, b_ref, s0_ref, y_ref, sout_ref, s_scr,
                   *, c, nc, fine):
    ch = pl.program_id(1)

    @pl.when(ch == 0)
    def _():
        s_scr[...] = s0_ref[0]

    mm = _dot3 if fine else _dotb
    mm_nt = _dot3_nt if fine else _dotb_nt
    mm_tn = _dot3_tn if fine else _dotb_tn
    gsz = 4
    gl = gsz * RW_HEAD
    rows = gsz * c
    sh = int(math.log2(c))
    ri = _iota2((rows, rows), 0)
    ci = _iota2((rows, rows), 1)
    same = (ri >> sh) == (ci >> sh)
    incl = jnp.where(same, jnp.where(ci <= ri, 1.0, 0.0), 0.0)
    strict = jnp.where(same, jnp.where(ci < ri, 1.0, 0.0), 0.0)
    rc = _iota2((c, c), 0)
    cc = _iota2((c, c), 1)
    ltri = jnp.where(cc <= rc, 1.0, 0.0).astype(F32)
    mine = (_iota2((rows, gl), 0) >> sh) == (_iota2((rows, gl), 1) >> 6)
    lane_head = _iota2((c, gl), 1) >> 6
    r2 = _iota2((gl, gl), 0)
    c2 = _iota2((gl, gl), 1)
    head_diag = (r2 >> 6) == (c2 >> 6)
    tile = lambda a: jnp.concatenate([a] * gsz, axis=0)
    for gi in range(RW_HEADS // gsz):
        sl = slice(gi * gl, (gi + 1) * gl)
        lw = lw_ref[:, sl]
        cum = _dot(ltri, lw, HIGHEST)
        ec = jnp.exp(cum)
        em = jnp.exp(-cum)
        bvec = b_ref[:, sl]
        kmod = km_ref[:, sl]
        v = v_ref[:, sl]
        clast = cum[c - 1:c, :]
        ef = jnp.exp(clast - cum)
        rt_s = jnp.where(mine, tile(r_ref[:, sl] * ec), 0.0)
        at_s = jnp.where(mine, tile(-kk_ref[:, sl] * jnp.exp(cum - lw)), 0.0)
        bt_t = tile(bvec * em)
        kt_t = tile(kmod * em)
        v_t = tile(v)
        a_ab = strict * mm_nt(at_s, bt_t)
        a_ak = strict * mm_nt(at_s, kt_t)
        a_rb = incl * mm_nt(rt_s, bt_t)
        a_rk = incl * mm_nt(rt_s, kt_t)
        xd = _tri_inv(-a_ab, rows, c, mm)
        s_lo = s_scr[2 * gi]
        s_hi = s_scr[2 * gi + 1]
        times_state = lambda a: jnp.concatenate([mm(a[:, :LANES], s_lo), mm(a[:, LANES:], s_hi)], axis=1)
        rhs = times_state(at_s) + mm(a_ak, v_t)
        u = rhs + mm(xd, rhs)
        y_st = times_state(rt_s) + mm(a_rb, u) + mm(a_rk, v_t)
        y = jnp.zeros((c, gl), F32)
        for g in range(gsz):
            y = jnp.where(lane_head == g, y_st[g * c:(g + 1) * c, :], y)
        y_ref[:, sl] = y
        pcol = jnp.sum(jnp.where(r2 == c2, jnp.broadcast_to(jnp.exp(clast), (gl, gl)), 0.0),
                       axis=1, keepdims=True)
        upd = mm_tn(jnp.where(mine, tile(bvec * ef), 0.0), u) + mm_tn(kmod * ef, v)
        upd = jnp.where(head_diag, upd, 0.0)
        s_scr[2 * gi] = s_lo * pcol[:LANES, :] + upd[:LANES, :LANES]
        s_scr[2 * gi + 1] = s_hi * pcol[LANES:, :] + upd[LANES:, LANES:]

    @pl.when(ch == nc - 1)
    def _():
        sout_ref[0] = s_scr[...]


def _rw_chunk(r, lw, km, v, kk, bv, s0, *, bsz, tp, c, fine):
    nc = tp // c
    n = bsz * tp
    row = lambda b, j: (b * nc + j, 0)
    blk = pl.BlockSpec((c, D_MODEL), row)
    st = pl.BlockSpec((1, RW_HEADS // 2, LANES, LANES), lambda b, j: (b, 0, 0, 0))
    body = functools.partial(_rw_chunk_body, c=c, nc=nc, fine=fine)
    return pl.pallas_call(
        body, grid=(bsz, nc),
        in_specs=[blk] * 6 + [st],
        out_specs=[blk, st],
        out_shape=[_sds((n, D_MODEL)), _sds((bsz, RW_HEADS // 2, LANES, LANES))],
        scratch_shapes=[pltpu.VMEM((RW_HEADS // 2, LANES, LANES), F32)],
        compiler_params=_cparams(("parallel", "arbitrary")), name="rw_chunk",
    )(r, lw, km, v, kk, bv, s0)


def _rw_out_body(y_ref, r_ref, km_ref, v_ref, g_ref, x_ref, vec_ref, bd_ref, w_ref, o_ref):
    y = y_ref[...]
    inv = 1.0 / RW_HEAD
    mean = _dot_exact_rhs(y, bd_ref[...]) * inv
    d = y - mean
    var = _dot_exact_rhs(d * d, bd_ref[...]) * inv
    yn = d * lax.rsqrt(var + RW_GN_EPS) * vec_ref[4:5, :] + vec_ref[5:6, :]
    bonus = _dot_exact_rhs(r_ref[...] * km_ref[...] * vec_ref[6:7, :], bd_ref[...]) * v_ref[...]
    o_ref[...] = x_ref[...] + _dot(((yn + bonus) * g_ref[...]).astype(BF16), w_ref[...])


def _rw_out(y, r, km, v, g, x, vec, bd, w_out, tm):
    n = y.shape[0]
    blk = pl.BlockSpec((tm, D_MODEL), lambda i: (i, 0))
    return pl.pallas_call(
        _rw_out_body, grid=(n // tm,),
        in_specs=[blk] * 6 + [_full(vec), _full(bd), _full(w_out)],
        out_specs=blk, out_shape=_sds((n, D_MODEL)),
        compiler_params=_cparams(("parallel",)), name="rw_out",
    )(y, r, km, v, g, x, vec, bd, w_out)


def _rwkv_layer(x, shift0, s0, p, *, bsz, tp, tt, t_valid, chunk, fine, tm):
    r, lw, km, v, kk, bv, g, hl = _rw_pre(x, shift0, p, bsz=bsz, tp=tp, tt=tt, t_valid=t_valid)
    y, s_new = _rw_chunk(r, lw, km, v, kk, bv, s0, bsz=bsz, tp=tp, c=chunk, fine=fine)
    out = _rw_out(y, r, km, v, g, x, p["vec"], p["bd"], p["w_out"], tm)
    return out, hl[:, SUBLANES - 1, :], s_new


def _top_values(arrs, dsts, k):
    idxs = [_iota2(a.shape, 0) for a in arrs]

    def step(i, carry):
        out = []
        for s, dst, idx in zip(carry, dsts, idxs):
            m = jnp.max(s, axis=0, keepdims=True)
            dst[pl.ds(i, 1), :] = m
            first = jnp.min(jnp.where(s == m, idx, s.shape[0]), axis=0, keepdims=True)
            out.append(jnp.where(idx == first, NEG_BIG, s))
        return tuple(out)

    lax.fori_loop(0, k, step, tuple(arrs))


def _pair_candidates(top1, top2):
    t2_8 = top2[0:SUBLANES, :]
    sub = _iota2(t2_8.shape, 0)
    pieces = [top1[0:1, :] + top2[...], top1[1:2, :] + t2_8]
    for a in range(2, SUBLANES):
        pieces.append(jnp.where(sub < PEER_TOPK // (a + 1), top1[a:a + 1, :] + t2_8, NEG_BIG))
    pieces.append(top1[SUBLANES:PEER_TOPK, :] + top2[0:1, :])
    return jnp.concatenate(pieces, axis=0)


def _peer_route_body(x_ref, ln_ref, wqh_ref, wql_ref, skh_ref, skl_ref,
                     xn_ref, s1_ref, s2_ref, e1_ref, e2_ref, thr_ref,
                     qh_s, ql_s, top_s, best_s):
    xn = _rms(x_ref[...], ln_ref[...])
    xn_ref[...] = xn.astype(BF16)
    xh, xl = _split(xn)
    q_t = _dot_nt(wqh_ref[...], xh) + (_dot_nt(wqh_ref[...], xl) + _dot_nt(wql_ref[...], xh))
    qh, ql = _split(q_t)
    qh_s[...] = qh
    ql_s[...] = ql

    def head_pair(hp, _):
        scores = []
        for e in range(2):
            for p, s_ref in enumerate((s1_ref, s2_ref)):
                idx = 2 * (2 * hp + e) + p
                r0 = pl.multiple_of(idx * PEER_HALF, PEER_HALF)
                skh = skh_ref[idx]
                qhh = qh_s[pl.ds(r0, PEER_HALF), :]
                s = _dot(skh, qhh) + (_dot(skh, ql_s[pl.ds(r0, PEER_HALF), :]) + _dot(skl_ref[idx], qhh))
                s_ref[2 * hp + e] = s
                scores.append(s)
        _top_values(scores, [top_s.at[i] for i in range(4)], PEER_TOPK)
        cands = [_pair_candidates(top_s.at[2 * e], top_s.at[2 * e + 1]) for e in range(2)]
        _top_values(cands, [best_s.at[e] for e in range(2)], PEER_TOPK)
        for e in range(2):
            h = 2 * hp + e
            best = best_s[e]
            zsum = jnp.sum(jnp.exp(best - best[0:1, :]), axis=0, keepdims=True)
            thr_ref[pl.ds(h, 1), :] = best[PEER_TOPK - 1:PEER_TOPK, :]
            e1_ref[h] = jnp.exp(s1_ref[h] - top_s[2 * e, 0:1, :]) / zsum
            e2_ref[h] = jnp.exp(s2_ref[h] - top_s[2 * e + 1, 0:1, :])
        return 0

    lax.fori_loop(0, PEER_HEADS // 2, head_pair, 0)


def _peer_route(x, ln, wq_hi, wq_lo, sk_hi, sk_lo, tm):
    n = x.shape[0]
    body = _peer_route_body
    sc = pl.BlockSpec((PEER_HEADS, PEER_NKEYS, tm), lambda i: (0, 0, i))
    return pl.pallas_call(
        body, grid=(n // tm,),
        scratch_shapes=[pltpu.VMEM((D_MODEL, tm), BF16), pltpu.VMEM((D_MODEL, tm), BF16),
                        pltpu.VMEM((4, PEER_TOPK, tm), F32), pltpu.VMEM((2, PEER_TOPK, tm), F32)],
        in_specs=[pl.BlockSpec((tm, D_MODEL), lambda i: (i, 0)), _full(ln), _full(wq_hi), _full(wq_lo),
                  _full(sk_hi), _full(sk_lo)],
        out_specs=[pl.BlockSpec((tm, D_MODEL), lambda i: (i, 0)), sc, sc, sc, sc,
                   pl.BlockSpec((PEER_HEADS, tm), lambda i: (0, i))],
        out_shape=[_sds((n, D_MODEL), BF16)] + [_sds((PEER_HEADS, PEER_NKEYS, n))] * 4
                  + [_sds((PEER_HEADS, n))],
        compiler_params=_cparams(("parallel",)), name="peer_route",
    )(x, ln, wq_hi, wq_lo, sk_hi, sk_lo)


def _gelu(a):
    return 0.5 * a * (1.0 + lax.erf(a * (2.0 ** -0.5)))


def _peer_gate(act_ref, p_ref, s1_ref, s2_ref, e1_ref, e2_ref, thr_ref, base, lane_blocks, *, ncb):
    for lb in lane_blocks:
        ls = slice(lb * LANES, (lb + 1) * LANES)
        s1t =[s1_ref[h, pl.ds(base, ncb), ls] for h in range(PEER_HEADS)]
        e1t = [e1_ref[h, pl.ds(base, ncb), ls] for h in range(PEER_HEADS)]
        thr = thr_ref[:, ls]
        for cc in range(ncb):
            gate = jnp.zeros((PEER_NKEYS, LANES), F32)
            for h in range(PEER_HEADS):
                ssum = s1t[h][cc:cc + 1, :] + s2_ref[h, :, ls]
                sel = jnp.where(ssum >= thr[h:h + 1, :], e2_ref[h, :, ls], 0.0)
                gate = gate + sel * e1t[h][cc:cc + 1, :]
            rs = slice(cc * PEER_NKEYS, (cc + 1) * PEER_NKEYS)
            p_ref[rs, ls] = (gate * _gelu(act_ref[rs, ls])).astype(BF16)


def _peer_main_body(xn_ref, x_ref, s1_ref, s2_ref, e1_ref, e2_ref, thr_ref, u0_ref, ub_ref, una_ref,
                    vt_ref, o_ref, act_a, act_b, p_a, p_b, acc_s, *, tm, te):
    j = pl.program_id(1)
    ncb = te // PEER_NKEYS
    base_a = pl.multiple_of(j * (2 * ncb), SUBLANES)
    base_b = pl.multiple_of(j * (2 * ncb) + ncb, SUBLANES)
    gate = functools.partial(_peer_gate, s1_ref=s1_ref, s2_ref=s2_ref, e1_ref=e1_ref, e2_ref=e2_ref,
                             thr_ref=thr_ref, ncb=ncb)
    tc = min(tm, MXU_DIM)
    chunks = [(slice(c0, c0 + tc), range(c0 // LANES, (c0 + tc) // LANES)) for c0 in range(0, tm, tc)]

    @pl.when(j == 0)
    def _():
        acc_s[...] = jnp.zeros_like(acc_s)
        act_a[...] = _dot_nt(u0_ref[...], xn_ref[...])

    for ts, lbs in chunks:
        act_b[:, ts] = _dot_nt(ub_ref[...], xn_ref[ts, :])
        gate(act_a, p_a, base=base_a, lane_blocks=lbs)
    for ts, lbs in chunks:
        acc_s[:, ts] += _dot(vt_ref[:, 0:te], p_a[:, ts])
        act_a[:, ts] = _dot_nt(una_ref[...], xn_ref[ts, :])
        gate(act_b, p_b, base=base_b, lane_blocks=lbs)
    for ts, lbs in chunks:
        acc_s[:, ts] += _dot(vt_ref[:, te:2 * te], p_b[:, ts])

    @pl.when(j == pl.num_programs(1) - 1)
    def _():
        o_ref[...] = x_ref[...] + acc_s[...].T


def _peer_main(xn, x, s1, s2, e1, e2, thr, u_bf16, vt_bf16, tm, te):
    n = x.shape[0]
    n_exp = u_bf16.shape[0]
    assert te % (SUBLANES * PEER_NKEYS) == 0 and n_exp % (2 * te) == 0 and n % tm == 0
    nj = n_exp // (2 * te)
    body = functools.partial(_peer_main_body, tm=tm, te=te)
    once = pl.Buffered(1)
    sc = pl.BlockSpec((PEER_HEADS, PEER_NKEYS, tm), lambda i, j: (0, 0, i), pipeline_mode=once)
    tok = pl.BlockSpec((tm, D_MODEL), lambda i, j: (i, 0))
    return pl.pallas_call(
        body, grid=(n // tm, nj),
        in_specs=[tok, tok, sc, sc, sc, sc,
                  pl.BlockSpec((PEER_HEADS, tm), lambda i, j: (0, i)),
                  pl.BlockSpec((te, D_MODEL), lambda i, j: (0, 0), pipeline_mode=once),
                  pl.BlockSpec((te, D_MODEL), lambda i, j: (2 * j + 1, 0)),
                  pl.BlockSpec((te, D_MODEL), lambda i, j: (jnp.minimum(2 * j + 2, 2 * nj - 1), 0)),
                  pl.BlockSpec((D_MODEL, 2 * te), lambda i, j: (0, j))],
        out_specs=pl.BlockSpec((tm, D_MODEL), lambda i, j: (i, 0)), out_shape=_sds((n, D_MODEL)),
        scratch_shapes=[pltpu.VMEM((te, tm), F32), pltpu.VMEM((te, tm), F32),
                        pltpu.VMEM((te, tm), BF16), pltpu.VMEM((te, tm), BF16),
                        pltpu.VMEM((D_MODEL, tm), F32)],
        compiler_params=_cparams(("parallel", "arbitrary")), name="peer_main",
    )(xn, x, s1, s2, e1, e2, thr, u_bf16, u_bf16, u_bf16, vt_bf16)


def _peer_layer(x, p, tm, te):
    xn, s1, s2, e1, e2, thr = _peer_route(x, p["ln"], p["wq_hi"], p["wq_lo"], p["sk_hi"], p["sk_lo"],
                                          min(tm, 256))
    return _peer_main(xn, x, s1, s2, e1, e2, thr, p["u"], p["vt"], tm, te)


def _row(v):
    return v.reshape(1, -1).astype(F32)


def _pad_rows(rows, total=SUBLANES):
    m = jnp.stack(rows).astype(F32)
    return jnp.pad(m, ((0, total - m.shape[0]), (0, 0)))


def _blockdiag_ones(width, block):
    i = jnp.arange(width) // block
    return (i[:, None] == i[None, :]).astype(BF16)


def _prep_dn(j, ln, dn_w_in, dn_conv_w, dn_A_log, dn_dt_bias, dn_o_norm, dn_w_out):
    pad8 = lambda v: jnp.pad(v.astype(F32), (0, LANES - DN_HEADS))
    return dict(
        ln=_row(ln),
        w_in=jnp.pad(dn_w_in[j], ((0, 0), (0, DN_PROJ_PAD - DN_PROJ))).astype(BF16),
        conv_w=_pad_rows(list(dn_conv_w[j])),
        pa=_pad_rows([pad8(dn_A_log[j]), pad8(dn_dt_bias[j])]),
        o_norm=_row(dn_o_norm[j]),
        w_out=dn_w_out[j].astype(BF16))


def _prep_rw(j, ln, rw):
    (mu, w_r, w_k, w_v, w0, w1, w2, a0, a1, a2, g1, g2, k_k, k_a, r_k, lnx_w, lnx_b, w_out) = rw
    bf = lambda w: w[j].astype(BF16)
    return dict(
        ln=_row(ln), mu=_pad_rows(list(mu[j])),
        w_r=bf(w_r), w_k=bf(w_k), w_v=bf(w_v), w1=bf(w1), w2=bf(w2), a1=bf(a1), a2=bf(a2),
        g1=bf(g1), g2=bf(g2),
        vec=_pad_rows([w0[j], a0[j], k_k[j], k_a[j], lnx_w[j], lnx_b[j], r_k[j].reshape(-1)]),
        bd=_blockdiag_ones(D_MODEL, RW_HEAD), w_out=bf(w_out))


def _prep_peer(i, ln, peer_w_q, peer_subkeys, peer_u, peer_v):
    wq_t = peer_w_q[i].T
    wq_hi = wq_t.astype(BF16)
    wq_lo = (wq_t - wq_hi.astype(F32)).astype(BF16)
    sk = peer_subkeys[i].reshape(2 * PEER_HEADS, PEER_NKEYS, PEER_HALF)
    sk_hi = sk.astype(BF16)
    sk_lo = (sk - sk_hi.astype(F32)).astype(BF16)
    return dict(ln=_row(ln), wq_hi=wq_hi, wq_lo=wq_lo, sk_hi=sk_hi, sk_lo=sk_lo,
                u=peer_u[i].astype(BF16), vt=peer_v[i].T.astype(BF16))


def _pair_states(s):
    b = s.shape[0]
    st = jnp.swapaxes(s, -1, -2).reshape(b, RW_HEADS // 2, 2, RW_HEAD, RW_HEAD)
    z = jnp.zeros_like(st[:, :, 0])
    top = jnp.concatenate([st[:, :, 0], z], axis=-1)
    bot = jnp.concatenate([z, st[:, :, 1]], axis=-1)
    return jnp.concatenate([top, bot], axis=-2)


def _unpair_states(sp):
    b = sp.shape[0]
    h0 = sp[:, :, :RW_HEAD, :RW_HEAD]
    h1 = sp[:, :, RW_HEAD:, RW_HEAD:]
    st = jnp.stack([h0, h1], axis=2).reshape(b, RW_HEADS, RW_HEAD, RW_HEAD)
    return jnp.swapaxes(st, -1, -2)


def _trunk(x, dn_state, dn_conv, rw_state, rw_shift, sb_past, params, cfg):
    bsz, tp, t_valid = cfg["bsz"], cfg["tp"], cfg["t_valid"]
    tm = cfg["tm"]
    outs = dict(dn_s=[], dn_c=[], k=[], v=[], rw_s=[], rw_x=[])
    for i in range(4):
        j = i // 3
        kind = i % 3
        if kind == 0:
            c0 = jnp.pad(dn_conv[j], ((0, 0), (SUBLANES - 3, 0), (0, 0)))
            x, tail, s_new = _delta_layer(x, c0, dn_state[j], params["dn"][j], bsz=bsz, tp=tp,
                                          tt=cfg["tt"], t_valid=t_valid, chunk=cfg["dn_chunk"],
                                          fine=cfg["fine"], tm=tm)
            outs["dn_c"].append(tail)
            outs["dn_s"].append(s_new)
        elif kind == 1:
            p = params["sb"][j]
            q, k, v, qb, kb, vb = _sb_qkv(x, p["ln"], p["w_qkv"], tm)
            if sb_past is None:
                o = _sb_attn(qb, kb, vb, p["bias"], bsz=bsz, t=tp, tq=cfg["tq"], tk=cfg["tk"], qs=cfg["qs"])
            else:
                cache_k, cache_v, page_table = sb_past
                q3 = q.reshape(bsz, tp, D_MODEL)[:, :t_valid]
                hm = p["headmask"]
                qbd = (q3[:, :, None, :] * hm[None, None]).reshape(bsz, t_valid * SB_HEADS, D_MODEL)
                padk = lambda a: jnp.pad(a.reshape(bsz, tp, D_MODEL), ((0, 0), (0, PAGE_SIZE - tp), (0, 0)))
                o = _sb_decode(qbd, p["biasrows"], jnp.tile(hm, (t_valid, 1)), padk(k), padk(v),
                               cache_k[j].reshape(-1, SB_DH), cache_v[j].reshape(-1, SB_DH), page_table,
                               t_new=t_valid, ppb=4)
                o = o.reshape(bsz * tp, D_MODEL)
            x = _mm_res(o, x, p["w_out"], tm)
            outs["k"].append(k.reshape(bsz, tp, SB_HEADS, SB_DH)[:, :t_valid])
            outs["v"].append(v.reshape(bsz, tp, SB_HEADS, SB_DH)[:, :t_valid])
        else:
            sh0 = jnp.pad(rw_shift[j][:, None, :], ((0, 0), (SUBLANES - 1, 0), (0, 0)))
            x, hl, s_new = _rwkv_layer(x, sh0, _pair_states(rw_state[j]), params["rw"][j], bsz=bsz,
                                       tp=tp, tt=cfg["tt"], t_valid=t_valid, chunk=cfg["rw_chunk"],
                                       fine=cfg["fine"], tm=tm)
            outs["rw_x"].append(hl)
            outs["rw_s"].append(_unpair_states(s_new))
        x = _peer_layer(x, params["peer"][i], cfg["peer_tm"], cfg["peer_te"])
    y = _final_norm(x, params["ln_final"], tm)
    return y, outs


def kernel(x_prompt, x_sample, state_dn, state_dn_conv, cache_k, cache_v, page_table, state_wkv, state_shift, ln_mix, ln_ffn, ln_final, dn_w_in, dn_conv_w, dn_A_log, dn_dt_bias, dn_o_norm, dn_w_out, sb_w_qkv, sb_bias, sb_w_out, rw_mu, rw_w_r, rw_w_k, rw_w_v, rw_w0, rw_w1, rw_w2, rw_a0, rw_a1, rw_a2, rw_g1, rw_g2, rw_k_k, rw_k_a, rw_r_k, rw_lnx_w, rw_lnx_b, rw_w_out, peer_w_q, peer_subkeys, peer_u, peer_v):
    bsz, seq, _ = x_prompt.shape
    dbsz, dseq, _ = x_sample.shape
    dtp = SUBLANES
    rw = (rw_mu, rw_w_r, rw_w_k, rw_w_v, rw_w0, rw_w1, rw_w2, rw_a0, rw_a1, rw_a2, rw_g1, rw_g2,
          rw_k_k, rw_k_a, rw_r_k, rw_lnx_w, rw_lnx_b, rw_w_out)
    headmask = (jnp.arange(D_MODEL)[None, :] // SB_DH == jnp.arange(SB_HEADS)[:, None]).astype(F32)
    params = dict(
        dn=[_prep_dn(j, ln_mix[3 * j], dn_w_in, dn_conv_w, dn_A_log, dn_dt_bias, dn_o_norm, dn_w_out)
            for j in range(2)],
        sb=[dict(ln=_row(ln_mix[1]), w_qkv=sb_w_qkv[0].astype(BF16), bias=sb_bias[0].astype(F32),
                 biasrows=jnp.broadcast_to(jnp.tile(sb_bias[0].astype(F32), dseq)[:, None],
                                           (dseq * SB_HEADS, LANES)),
                 headmask=headmask, w_out=sb_w_out[0].astype(BF16))],
        rw=[_prep_rw(0, ln_mix[2], rw)],
        peer=[_prep_peer(i, ln_ffn[i], peer_w_q, peer_subkeys, peer_u, peer_v) for i in range(4)],
        ln_final=_row(ln_final))

    cfg_p = dict(bsz=bsz, tp=seq, t_valid=seq, tt=256, tm=256, dn_chunk=DN_CHUNK, rw_chunk=RW_CHUNK,
                 fine=False, tq=512, tk=512, qs=128, peer_tm=512, peer_te=1024)
    zeros = lambda *s: jnp.zeros(s, F32)
    y_p, o_p = _trunk(x_prompt.reshape(bsz * seq, D_MODEL),
                      zeros(2, bsz, DN_HEADS, DN_DK, DN_DK), zeros(2, bsz, DN_CONV - 1, 3 * DN_WIDTH),
                      zeros(1, bsz, RW_HEADS, RW_HEAD, RW_HEAD), zeros(1, bsz, D_MODEL), None,
                      params, cfg_p)

    cfg_s = dict(bsz=dbsz, tp=dtp, t_valid=dseq, tt=dtp, tm=dbsz * dtp, dn_chunk=dtp, rw_chunk=dtp,
                 fine=True, peer_tm=dbsz * dtp, peer_te=1024)
    xs = jnp.pad(x_sample, ((0, 0), (0, dtp - dseq), (0, 0))).reshape(dbsz * dtp, D_MODEL)
    y_s, o_s = _trunk(xs, state_dn, state_dn_conv, state_wkv, state_shift,
                      (cache_k, cache_v, page_table), params, cfg_s)

    def pack(y, o, b, tp, t):
        return (y.reshape(b, tp, D_MODEL)[:, :t], jnp.stack(o["dn_s"]), jnp.stack(o["dn_c"]),
                jnp.stack(o["k"]), jnp.stack(o["v"]), jnp.stack(o["rw_s"]), jnp.stack(o["rw_x"]))

    pp = pack(y_p, o_p, bsz, seq, seq)
    ps = pack(y_s, o_s, dbsz, dtp, dseq)
    return (pp[0], ps[0]) + pp[1:] + ps[1:]
```

```python
import functools
import math

import jax
import jax.numpy as jnp
from jax import lax
from jax.experimental import pallas as pl
from jax.experimental.pallas import tpu as pltpu

F32 = jnp.float32
BF16 = jnp.bfloat16
HIGHEST = lax.Precision.HIGHEST

D_MODEL = 1024
RMS_EPS = 1e-6
LANES = 128
SUBLANES = 8
MXU_DIM = 256
VMEM_LIMIT = 48 * 1024 * 1024

DN_HEADS = 8
DN_DK = 128
DN_WIDTH = 1024
DN_CONV = 4
DN_CHUNK = 64
DN_PROJ = 4 * DN_WIDTH + 2 * DN_HEADS
DN_PROJ_PAD = 4 * DN_WIDTH + LANES

SB_HEADS = 16
SB_DH = 64
PAGE_SIZE = 128

RW_HEAD = 64
RW_HEADS = 16
RW_GN_EPS = 64e-5
RW_CHUNK = 64

PEER_HEADS = 8
PEER_NKEYS = 128
PEER_HALF = 64
PEER_TOPK = 16
NEG_BIG = -3.0e38


def _dot(a, b, precision=None):
    return jnp.dot(a, b, preferred_element_type=F32, precision=precision)


def _dot_nt(a, b, precision=None):
    return lax.dot_general(a, b, (((1,), (1,)), ((), ())), preferred_element_type=F32,
                           precision=precision)


def _split(a):
    hi = a.astype(BF16)
    lo = (a - hi.astype(F32)).astype(BF16)
    return hi, lo


def _dot3(a, b):
    ah, al = _split(a)
    bh, bl = _split(b)
    return _dot(ah, bh) + (_dot(ah, bl) + _dot(al, bh))


def _dot3_nt(a, b):
    ah, al = _split(a)
    bh, bl = _split(b)
    return _dot_nt(ah, bh) + (_dot_nt(ah, bl) + _dot_nt(al, bh))


def _dot_tn(a, b):
    return lax.dot_general(a, b, (((0,), (0,)), ((), ())), preferred_element_type=F32)


def _dot3_tn(a, b):
    ah, al = _split(a)
    bh, bl = _split(b)
    return _dot_tn(ah, bh) + (_dot_tn(ah, bl) + _dot_tn(al, bh))


def _dotb_tn(a, b):
    return _dot_tn(a.astype(BF16), b.astype(BF16))


def _dotb(a, b):
    return _dot(a.astype(BF16), b.astype(BF16))


def _dotb_nt(a, b):
    return _dot_nt(a.astype(BF16), b.astype(BF16))


def _dot_exact_rhs(a, b_exact):
    ah, al = _split(a)
    return _dot(ah, b_exact) + _dot(al, b_exact)


def _rms(x, g):
    return x * lax.rsqrt(jnp.mean(x * x, axis=-1, keepdims=True) + RMS_EPS) * g


def _sigmoid(x):
    return 1.0 / (1.0 + jnp.exp(-x))


def _softplus(x):
    return jnp.maximum(x, 0.0) + jnp.log1p(jnp.exp(-jnp.abs(x)))


def _iota2(shape, axis):
    return lax.broadcasted_iota(jnp.int32, shape, axis)


def _tri_inv(m, n, c, mm):
    ri = _iota2((n, n), 0)
    ci = _iota2((n, n), 1)
    m8 = jnp.where((ri >> 3) == (ci >> 3), m, 0.0)
    m8_2 = mm(m8, m8)
    m8_4 = mm(m8_2, m8_2)
    d = (m8_2 - m8) - mm(m8, m8_2)
    d = d + m8_4 + mm(d, m8_4)
    size, shift = 8, 3
    while size < c:
        lowleft = jnp.where((ri >> (shift + 1)) == (ci >> (shift + 1)),
                            jnp.where(((ri >> shift) & 1) == 1,
                                      jnp.where(((ci >> shift) & 1) == 0, 1.0, 0.0), 0.0), 0.0)
        cm = m * lowleft
        t = cm + mm(d, cm)
        d = d - (t + mm(t, d))
        size, shift = size * 2, shift + 1
    return d


def _full(a):
    nd = a.ndim
    return pl.BlockSpec(a.shape, lambda *_, _nd=nd: (0,) * _nd)


def _cparams(sem):
    return pltpu.CompilerParams(dimension_semantics=sem, vmem_limit_bytes=VMEM_LIMIT)


def _sds(shape, dtype=F32):
    return jax.ShapeDtypeStruct(shape, dtype)


def _mm_res_body(a_ref, x_ref, w_ref, o_ref):
    o_ref[...] = x_ref[...] + _dot(a_ref[...].astype(BF16), w_ref[...])


def _mm_res(a, x, w_bf16, tm):
    n, k = a.shape
    m = w_bf16.shape[1]
    return pl.pallas_call(
        _mm_res_body, grid=(n // tm,),
        in_specs=[pl.BlockSpec((tm, k), lambda i: (i, 0)), pl.BlockSpec((tm, m), lambda i: (i, 0)),
                  _full(w_bf16)],
        out_specs=pl.BlockSpec((tm, m), lambda i: (i, 0)),
        out_shape=_sds((n, m)), compiler_params=_cparams(("parallel",)), name="mm_res",
    )(a, x, w_bf16)


def _final_norm_body(x_ref, g_ref, o_ref):
    o_ref[...] = _rms(x_ref[...], g_ref[...])


def _final_norm(x, g, tm):
    n = x.shape[0]
    return pl.pallas_call(
        _final_norm_body, grid=(n // tm,),
        in_specs=[pl.BlockSpec((tm, D_MODEL), lambda i: (i, 0)), _full(g)],
        out_specs=pl.BlockSpec((tm, D_MODEL), lambda i: (i, 0)),
        out_shape=_sds((n, D_MODEL)), compiler_params=_cparams(("parallel",)), name="final_norm",
    )(x, g)


def _dn_pre_body(x_ref, c0_ref, ln_ref, w_ref, cw_ref, pa_ref,
                 q_ref, k_ref, v_ref, z_ref, gc_ref, bt_ref, tail_ref,
                 proj_s, cbuf, *, tt, t_valid, nt, chunk):
    t = pl.program_id(1)
    h = _rms(x_ref[...], ln_ref[...])
    proj_s[...] = _dot(h.astype(BF16), w_ref[...])

    @pl.when(t == 0)
    def _():
        cbuf[0:SUBLANES, :] = c0_ref[0]

    cbuf[SUBLANES:SUBLANES + tt, :] = proj_s[:, 0:3 * DN_WIDTH]
    for cb in range(3 * DN_HEADS):
        sl = slice(cb * LANES, (cb + 1) * LANES)
        conv = (cbuf[5:5 + tt, sl] * cw_ref[0:1, sl] + cbuf[6:6 + tt, sl] * cw_ref[1:2, sl]
                + cbuf[7:7 + tt, sl] * cw_ref[2:3, sl] + cbuf[8:8 + tt, sl] * cw_ref[3:4, sl])
        a = conv * _sigmoid(conv)
        if cb < DN_HEADS:
            a = a * lax.rsqrt(jnp.sum(a * a, axis=-1, keepdims=True) + 1e-6) * (DN_DK ** -0.5)
            q_ref[:, sl] = a
        elif cb < 2 * DN_HEADS:
            a = a * lax.rsqrt(jnp.sum(a * a, axis=-1, keepdims=True) + 1e-6)
            k_ref[:, (cb - DN_HEADS) * LANES:(cb - DN_HEADS + 1) * LANES] = a
        else:
            v_ref[:, (cb - 2 * DN_HEADS) * LANES:(cb - 2 * DN_HEADS + 1) * LANES] = a
    z_ref[...] = proj_s[:, 3 * DN_WIDTH:4 * DN_WIDTH]

    last = proj_s[:, 4 * DN_WIDTH:DN_PROJ_PAD]
    g = -jnp.exp(pa_ref[0:1, :]) * _softplus(last + pa_ref[1:2, :])
    beta = _sigmoid(pltpu.roll(last, LANES - DN_HEADS, 1))
    row = t * tt + _iota2((tt, LANES), 0)
    lane = _iota2((tt, LANES), 1)
    keep = jnp.where(row < t_valid, jnp.where(lane < DN_HEADS, 1.0, 0.0), 0.0)
    g = g * keep
    bt_ref[...] = beta * keep
    ri = _iota2((tt, tt), 0)
    ci = _iota2((tt, tt), 1)
    sh = int(math.log2(chunk))
    ltri = jnp.where((ri >> sh) == (ci >> sh), jnp.where(ci <= ri, 1.0, 0.0), 0.0).astype(F32)
    gc_ref[...] = _dot(ltri, g, HIGHEST)

    tl = t_valid - (nt - 1) * tt

    @pl.when(t == nt - 1)
    def _():
        tail_ref[0] = cbuf[tl:tl + SUBLANES, :]

    cbuf[0:SUBLANES, :] = cbuf[tt:tt + SUBLANES, :]


def _dn_pre(x, conv0, ln, w_in, conv_w, pa, *, bsz, tp, tt, t_valid, chunk):
    nt = tp // tt
    n = bsz * tp
    row = lambda b, t: (b * nt + t, 0)
    body = functools.partial(_dn_pre_body, tt=tt, t_valid=t_valid, nt=nt, chunk=chunk)
    wide = pl.BlockSpec((tt, DN_WIDTH), row)
    narrow = pl.BlockSpec((tt, LANES), row)
    return pl.pallas_call(
        body, grid=(bsz, nt),
        in_specs=[wide, pl.BlockSpec((1, SUBLANES, 3 * DN_WIDTH), lambda b, t: (b, 0, 0)),
                  _full(ln), _full(w_in), _full(conv_w), _full(pa)],
        out_specs=[wide, wide, wide, wide, narrow, narrow,
                   pl.BlockSpec((1, SUBLANES, 3 * DN_WIDTH), lambda b, t: (b, 0, 0))],
        out_shape=[_sds((n, DN_WIDTH))] * 4 + [_sds((n, LANES))] * 2
                  + [_sds((bsz, SUBLANES, 3 * DN_WIDTH))],
        scratch_shapes=[pltpu.VMEM((tt, DN_PROJ_PAD), F32),
                        pltpu.VMEM((tt + 2 * SUBLANES, 3 * DN_WIDTH), F32)],
        compiler_params=_cparams(("parallel", "arbitrary")), name="dn_pre",
    )(x, conv0, ln, w_in, conv_w, pa)


def _dn_chunk_body(q_ref, k_ref, v_ref, gc_ref, bt_ref, s0_ref, o_ref, sout_ref, s_scr,
                   *, c, nc, fine):
    ch = pl.program_id(1)

    @pl.when(ch == 0)
    def _():
        s_scr[...] = s0_ref[0]

    mm = _dot3 if fine else _dotb
    mm_nt = _dot3_nt if fine else _dotb_nt
    mm_tn = _dot3_tn if fine else _dotb_tn
    gsz = min(DN_HEADS, MXU_DIM // c)
    rows = gsz * c
    sh = int(math.log2(c))
    ri = _iota2((rows, rows), 0)
    ci = _iota2((rows, rows), 1)
    same = (ri >> sh) == (ci >> sh)
    incl = jnp.where(same, jnp.where(ci <= ri, 1.0, 0.0), 0.0)
    strict = jnp.where(same, jnp.where(ci < ri, 1.0, 0.0), 0.0)
    for g0 in range(0, DN_HEADS, gsz):
        heads = range(g0, g0 + gsz)
        stack = lambda ref: jnp.concatenate([ref[:, h * LANES:(h + 1) * LANES] for h in heads], axis=0)
        col = lambda ref: jnp.concatenate([ref[:, h:h + 1] for h in heads], axis=0)
        q = stack(q_ref)
        k = stack(k_ref)
        v = stack(v_ref)
        gcol = col(gc_ref)
        bcol = col(bt_ref)
        glast = jnp.concatenate([jnp.broadcast_to(gc_ref[c - 1:c, h:h + 1], (c, 1)) for h in heads], axis=0)
        gmat = jnp.broadcast_to(gcol, (rows, rows))
        grow = jnp.sum(jnp.where(ri == ci, gmat, 0.0), axis=0, keepdims=True)
        decay = jnp.exp(jnp.where(incl > 0.0, gmat - grow, NEG_BIG))
        m = strict * (bcol * mm_nt(k, k) * decay)
        xd = _tri_inv(m, rows, c, mm)
        eg = jnp.exp(gcol)
        rhs_u = bcol * v
        rhs_w = (bcol * eg) * k
        u = rhs_u + mm(xd, rhs_u)
        w = rhs_w + mm(xd, rhs_w)
        qk = mm_nt(q, k) * decay
        qd = q * eg
        k_dec = k * jnp.exp(glast - gcol)
        states = [s_scr[h] for h in heads]
        blk = lambda a, i: a[i * c:(i + 1) * c, :]
        v_new = u - jnp.concatenate([mm(blk(w, i), states[i]) for i in range(gsz)], axis=0)
        o = jnp.concatenate([mm(blk(qd, i), states[i]) for i in range(gsz)], axis=0) + mm(qk, v_new)
        for i, h in enumerate(heads):
            o_ref[:, h * LANES:(h + 1) * LANES] = blk(o, i)
            s_scr[h] = (states[i] * jnp.exp(gc_ref[c - 1:c, h:h + 1])
                        + mm_tn(blk(k_dec, i), blk(v_new, i)))

    @pl.when(ch == nc - 1)
    def _():
        sout_ref[0] = s_scr[...]


def _dn_chunk(q, k, v, gc, bt, s0, *, bsz, tp, c, fine):
    nc = tp // c
    n = bsz * tp
    row = lambda b, j: (b * nc + j, 0)
    wide = pl.BlockSpec((c, DN_WIDTH), row)
    narrow = pl.BlockSpec((c, LANES), row)
    st = pl.BlockSpec((1, DN_HEADS, DN_DK, DN_DK), lambda b, j: (b, 0, 0, 0))
    body = functools.partial(_dn_chunk_body, c=c, nc=nc, fine=fine)
    return pl.pallas_call(
        body, grid=(bsz, nc),
        in_specs=[wide, wide, wide, narrow, narrow, st],
        out_specs=[wide, st],
        out_shape=[_sds((n, DN_WIDTH)), _sds((bsz, DN_HEADS, DN_DK, DN_DK))],
        scratch_shapes=[pltpu.VMEM((DN_HEADS, DN_DK, DN_DK), F32)],
        compiler_params=_cparams(("parallel", "arbitrary")), name="dn_chunk",
    )(q, k, v, gc, bt, s0)


def _dn_out_body(o_ref, z_ref, x_ref, on_ref, w_ref, y_ref):
    parts = []
    for h in range(DN_HEADS):
        sl = slice(h * LANES, (h + 1) * LANES)
        o = o_ref[:, sl]
        o = o * lax.rsqrt(jnp.mean(o * o, axis=-1, keepdims=True) + RMS_EPS) * on_ref[...]
        z = z_ref[:, sl]
        parts.append((o * (z * _sigmoid(z))).astype(BF16))
    y_ref[...] = x_ref[...] + _dot(jnp.concatenate(parts, axis=-1), w_ref[...])


def _dn_out(o, z, x, o_norm, w_out, tm):
    n = o.shape[0]
    blk = pl.BlockSpec((tm, D_MODEL), lambda i: (i, 0))
    return pl.pallas_call(
        _dn_out_body, grid=(n // tm,),
        in_specs=[blk, blk, blk, _full(o_norm), _full(w_out)],
        out_specs=blk, out_shape=_sds((n, D_MODEL)),
        compiler_params=_cparams(("parallel",)), name="dn_out",
    )(o, z, x, o_norm, w_out)


def _delta_layer(x, conv0, s0, p, *, bsz, tp, tt, t_valid, chunk, fine, tm):
    q, k, v, z, gc, bt, tail = _dn_pre(x, conv0, p["ln"], p["w_in"], p["conv_w"], p["pa"],
                                       bsz=bsz, tp=tp, tt=tt, t_valid=t_valid, chunk=chunk)
    o, s_new = _dn_chunk(q, k, v, gc, bt, s0, bsz=bsz, tp=tp, c=chunk, fine=fine)
    y = _dn_out(o, z, x, p["o_norm"], p["w_out"], tm)
    return y, tail[:, 5:8, :], s_new


def _sb_qkv_body(x_ref, ln_ref, w_ref, q_ref, k_ref, v_ref, qb_ref, kb_ref, vb_ref):
    h = _rms(x_ref[...], ln_ref[...]).astype(BF16)
    r = _dot(h, w_ref[...])
    q = r[:, 0:D_MODEL]
    k = r[:, D_MODEL:2 * D_MODEL]
    v = r[:, 2 * D_MODEL:3 * D_MODEL]
    q_ref[...] = q
    k_ref[...] = k
    v_ref[...] = v
    qb_ref[...] = (q * (SB_DH ** -0.5)).astype(BF16)
    kb_ref[...] = k.astype(BF16)
    vb_ref[...] = v.astype(BF16)


def _sb_qkv(x, ln, w, tm):
    n = x.shape[0]
    blk = pl.BlockSpec((tm, D_MODEL), lambda i: (i, 0))
    return pl.pallas_call(
        _sb_qkv_body, grid=(n // tm,),
        in_specs=[blk, _full(ln), _full(w)],
        out_specs=[blk] * 6, out_shape=[_sds((n, D_MODEL))] * 3 + [_sds((n, D_MODEL), BF16)] * 3,
        compiler_params=_cparams(("parallel",)), name="sb_qkv",
    )(x, ln, w)


def _later_matrix():
    j = _iota2((LANES, 2 * LANES), 0)
    s = _iota2((LANES, 2 * LANES), 1)
    return jnp.where(s >= LANES, 1.0, jnp.where(j > s, 1.0, 0.0)).astype(BF16)


def _sb_tile(z, mask, carry, uo):
    sp = jnp.maximum(z, 0.0) + jnp.log(1.0 + jnp.exp(-jnp.abs(z)))
    if mask is not None:
        sp = jnp.where(mask, sp, 0.0)
    cs = _dot(sp.astype(BF16), uo)
    a = jnp.exp(z - sp - (carry + cs[:, 0:LANES]))
    if mask is not None:
        a = jnp.where(mask, a, 0.0)
    return a, carry + cs[:, LANES:2 * LANES]


def _sb_attn_body(bias_ref, q_ref, k_ref, v_ref, o_ref, acc, car, *, tq, tk, qs):
    hp = pl.program_id(1)
    i = pl.program_id(2)
    jj = pl.program_id(3)
    kb = ((i + 1) * tq) // tk - 1 - jj

    @pl.when(jj == 0)
    def _():
        acc[...] = jnp.zeros_like(acc)
        car[...] = jnp.zeros_like(car)

    def run(masked):
        uo = _later_matrix()
        lane_row = _iota2((1, LANES), 1)
        lane = _iota2((tq, LANES), 1)
        q = q_ref[...]
        rel = (i * tq - kb * tk) + _iota2((tq, LANES), 0) - lane
        total = None
        for e in range(2):
            q_e = q * jnp.where((lane_row >> 6) == e, 1.0, 0.0).astype(BF16)
            carry = car[e]
            pv = jnp.zeros((tq, LANES), F32)
            for sub in reversed(range(tk // LANES)):
                ks = k_ref[sub * LANES:(sub + 1) * LANES, :]
                vs = v_ref[sub * LANES:(sub + 1) * LANES, :]
                z = _dot_nt(q_e, ks) + bias_ref[2 * hp + e]
                mask = (rel > sub * LANES) if masked else None
                a, carry = _sb_tile(z, mask, carry, uo)
                pv = pv + _dot(a.astype(BF16), vs)
            car[e] = carry
            total = pv if e == 0 else jnp.where((lane >> 6) == 0, total, pv)
        acc[...] += total

    touches_diagonal = (kb + 1) * tk > i * tq

    @pl.when(jnp.logical_and(kb >= 0, touches_diagonal))
    def _():
        run(True)

    @pl.when(jnp.logical_and(kb >= 0, jnp.logical_not(touches_diagonal)))
    def _():
        run(False)

    @pl.when(jj == pl.num_programs(3) - 1)
    def _():
        o_ref[...] = acc[...]


def _sb_attn(q, k, v, bias, *, bsz, t, tq, tk, qs):
    nq = t // tq
    nk = t // tk
    body = functools.partial(_sb_attn_body, tq=tq, tk=tk, qs=qs)

    def kmap(b, hp, i, jj):
        kb = jnp.maximum(((i + 1) * tq) // tk - 1 - jj, 0)
        return (b * nk + kb, hp)

    qspec = pl.BlockSpec((tq, LANES), lambda b, hp, i, jj: (b * nq + i, hp))
    kspec = pl.BlockSpec((tk, LANES), kmap)
    return pl.pallas_call(
        body, grid=(bsz, SB_HEADS // 2, nq, nk),
        in_specs=[pl.BlockSpec(memory_space=pltpu.SMEM), qspec, kspec, kspec],
        out_specs=qspec, out_shape=_sds((bsz * t, D_MODEL)),
        scratch_shapes=[pltpu.VMEM((tq, LANES), F32), pltpu.VMEM((2, tq, LANES), F32)],
        compiler_params=_cparams(("parallel", "parallel", "parallel", "arbitrary")), name="sb_attn",
    )(bias, q, k, v)


def _sb_dec_body(pt_ref, qbd_ref, bias_ref, hm_ref, kn_ref, vn_ref, *refs, t_new, ppb):
    kp_refs, vp_refs = refs[:ppb], refs[ppb:2 * ppb]
    o_ref, acc, car = refs[2 * ppb:]
    s = pl.program_id(1)
    rows = t_new * SB_HEADS
    uo = _later_matrix()
    qb = qbd_ref[0].astype(BF16)
    scale = SB_DH ** -0.5

    @pl.when(s == 0)
    def _():
        lane = _iota2((rows, LANES), 1)
        tq = _iota2((rows, LANES), 0) >> 4
        z = _dot_nt(qb, kn_ref[0].astype(BF16)) * scale + bias_ref[...]
        a, cnew = _sb_tile(z, lane < tq, jnp.zeros((rows, LANES), F32), uo)
        acc[...] = _dot(a.astype(BF16), vn_ref[0].astype(BF16))
        car[...] = cnew

    @pl.when(s > 0)
    def _():
        carry = car[...]
        total = acc[...]
        for r in range(ppb):
            z = _dot(qb, kp_refs[r][0].astype(BF16)) * scale + bias_ref[...]
            a, carry = _sb_tile(z, None, carry, uo)
            total = total + _dot_nt(a.astype(BF16), vp_refs[r][0].astype(BF16))
        car[...] = carry
        acc[...] = total

    @pl.when(s == pl.num_programs(1) - 1)
    def _():
        red = (acc[...] * hm_ref[...]).reshape(t_new, SB_HEADS, D_MODEL).sum(axis=1)
        o_ref[0] = jnp.concatenate([red, jnp.zeros((SUBLANES - t_new, D_MODEL), F32)], axis=0)


def _sb_decode(qbd, biasrows, headmask, k_new, v_new, cache_k, cache_v, page_table, *, t_new, ppb):
    bsz, n_pages = page_table.shape
    assert n_pages % ppb == 0
    rows = t_new * SB_HEADS
    body = functools.partial(_sb_dec_body, t_new=t_new, ppb=ppb)

    def pmap(r):
        def index(b, s, pt):
            return (pt[b * n_pages + (n_pages - 1) - (jnp.maximum(s - 1, 0) * ppb + r)], 0, 0)
        return index

    per_b3 = lambda shape: pl.BlockSpec(shape, lambda b, s, pt: (b, 0, 0))
    const2 = lambda a: pl.BlockSpec(a.shape, lambda b, s, pt: (0, 0))
    pages = [pl.BlockSpec((1, D_MODEL, PAGE_SIZE), pmap(r)) for r in range(ppb)]
    grid_spec = pltpu.PrefetchScalarGridSpec(
        num_scalar_prefetch=1, grid=(bsz, n_pages // ppb + 1),
        in_specs=[per_b3((1, rows, D_MODEL)), const2(biasrows), const2(headmask),
                  per_b3((1, PAGE_SIZE, D_MODEL)), per_b3((1, PAGE_SIZE, D_MODEL))] + pages + pages,
        out_specs=per_b3((1, SUBLANES, D_MODEL)),
        scratch_shapes=[pltpu.VMEM((rows, D_MODEL), F32), pltpu.VMEM((rows, LANES), F32)])
    return pl.pallas_call(
        body, grid_spec=grid_spec, out_shape=_sds((bsz, SUBLANES, D_MODEL)),
        compiler_params=_cparams(("parallel", "arbitrary")), name="sb_decode",
    )(page_table.reshape(-1), qbd, biasrows, headmask, k_new, v_new,
      *([cache_k] * ppb), *([cache_v] * ppb))


def _rw_pre_body(x_ref, sh0_ref, ln_ref, mu_ref, wr_ref, wk_ref, wv_ref, w1_ref, w2_ref,
                 a1_ref, a2_ref, g1_ref, g2_ref, vec_ref, bd_ref,
                 r_ref, lw_ref, km_ref, v_ref, kk_ref, b_ref, g_ref, hl_ref, hbuf,
                 *, tt, t_valid, nt):
    t = pl.program_id(1)
    h = _rms(x_ref[...], ln_ref[...])

    @pl.when(t == 0)
    def _():
        hbuf[0:SUBLANES, :] = sh0_ref[0]

    hbuf[SUBLANES:SUBLANES + tt, :] = h
    xx = hbuf[SUBLANES - 1:SUBLANES - 1 + tt, :] - h

    def mix(m):
        return (h + xx * mu_ref[m:m + 1, :]).astype(BF16)

    r_ref[...] = _dot(mix(0), wr_ref[...])
    wl = _dotb(jnp.tanh(_dot(mix(1), w1_ref[...])), w2_ref[...])
    w = -_softplus(-(vec_ref[0:1, :] + wl)) - 0.5
    k = _dot(mix(2), wk_ref[...])
    v_ref[...] = _dot(mix(3), wv_ref[...])
    a = _sigmoid(vec_ref[1:2, :] + _dotb(_dot(mix(4), a1_ref[...]), a2_ref[...]))
    g_ref[...] = _dotb(_sigmoid(_dot(mix(5), g1_ref[...])), g2_ref[...])

    row = t * tt + _iota2((tt, 1), 0)
    keep = jnp.where(row < t_valid, 1.0, 0.0)
    lw_ref[...] = -jnp.exp(w) * keep
    kt = k * vec_ref[2:3, :]
    ss = _dot_exact_rhs(kt * kt, bd_ref[...])
    kk = kt * lax.rsqrt(ss + 1e-6) * keep
    kk_ref[...] = kk
    b_ref[...] = kk * a
    km_ref[...] = k * (1.0 + (a - 1.0) * vec_ref[3:4, :]) * keep

    tl = t_valid - (nt - 1) * tt

    @pl.when(t == nt - 1)
    def _():
        hl_ref[0] = hbuf[tl:tl + SUBLANES, :]

    hbuf[0:SUBLANES, :] = hbuf[tt:tt + SUBLANES, :]


def _rw_pre(x, shift0, p, *, bsz, tp, tt, t_valid):
    nt = tp // tt
    n = bsz * tp
    row = lambda b, t: (b * nt + t, 0)
    blk = pl.BlockSpec((tt, D_MODEL), row)
    st = pl.BlockSpec((1, SUBLANES, D_MODEL), lambda b, t: (b, 0, 0))
    body = functools.partial(_rw_pre_body, tt=tt, t_valid=t_valid, nt=nt)
    consts = [p["ln"], p["mu"], p["w_r"], p["w_k"], p["w_v"], p["w1"], p["w2"], p["a1"], p["a2"],
              p["g1"], p["g2"], p["vec"], p["bd"]]
    return pl.pallas_call(
        body, grid=(bsz, nt),
        in_specs=[blk, st] + [_full(c) for c in consts],
        out_specs=[blk] * 7 + [st],
        out_shape=[_sds((n, D_MODEL))] * 7 + [_sds((bsz, SUBLANES, D_MODEL))],
        scratch_shapes=[pltpu.VMEM((tt + 2 * SUBLANES, D_MODEL), F32)],
        compiler_params=_cparams(("parallel", "arbitrary")), name="rw_pre",
    )(x, shift0, *consts)


def _rw_chunk_body(r_ref, lw_ref, km_ref, v_ref, kk_ref, b_ref, s0_ref, y_ref, sout_ref, s_scr,
                   *, c, nc, fine):
    ch = pl.program_id(1)

    @pl.when(ch == 0)
    def _():
        s_scr[...] = s0_ref[0]

    mm = _dot3 if fine else _dotb
    mm_nt = _dot3_nt if fine else _dotb_nt
    mm_tn = _dot3_tn if fine else _dotb_tn
    gsz = 4
    gl = gsz * RW_HEAD
    rows = gsz * c
    sh = int(math.log2(c))
    ri = _iota2((rows, rows), 0)
    ci = _iota2((rows, rows), 1)
    same = (ri >> sh) == (ci >> sh)
    incl = jnp.where(same, jnp.where(ci <= ri, 1.0, 0.0), 0.0)
    strict = jnp.where(same, jnp.where(ci < ri, 1.0, 0.0), 0.0)
    rc = _iota2((c, c), 0)
    cc = _iota2((c, c), 1)
    ltri = jnp.where(cc <= rc, 1.0, 0.0).astype(F32)
    mine = (_iota2((rows, gl), 0) >> sh) == (_iota2((rows, gl), 1) >> 6)
    lane_head = _iota2((c, gl), 1) >> 6
    r2 = _iota2((gl, gl), 0)
    c2 = _iota2((gl, gl), 1)
    head_diag = (r2 >> 6) == (c2 >> 6)
    tile = lambda a: jnp.concatenate([a] * gsz, axis=0)
    for gi in range(RW_HEADS // gsz):
        sl = slice(gi * gl, (gi + 1) * gl)
        lw = lw_ref[:, sl]
        cum = _dot(ltri, lw, HIGHEST)
        ec = jnp.exp(cum)
        em = jnp.exp(-cum)
        bvec = b_ref[:, sl]
        kmod = km_ref[:, sl]
        v = v_ref[:, sl]
        clast = cum[c - 1:c, :]
        ef = jnp.exp(clast - cum)
        rt_s = jnp.where(mine, tile(r_ref[:, sl] * ec), 0.0)
        at_s = jnp.where(mine, tile(-kk_ref[:, sl] * jnp.exp(cum - lw)), 0.0)
        bt_t = tile(bvec * em)
        kt_t = tile(kmod * em)
        v_t = tile(v)
        a_ab = strict * mm_nt(at_s, bt_t)
        a_ak = strict * mm_nt(at_s, kt_t)
        a_rb = incl * mm_nt(rt_s, bt_t)
        a_rk = incl * mm_nt(rt_s, kt_t)
        xd = _tri_inv(-a_ab, rows, c, mm)
        s_lo = s_scr[2 * gi]
        s_hi = s_scr[2 * gi + 1]
        times_state = lambda a: jnp.concatenate([mm(a[:, :LANES], s_lo), mm(a[:, LANES:], s_hi)], axis=1)
        rhs = times_state(at_s) + mm(a_ak, v_t)
        u = rhs + mm(xd, rhs)
        y_st = times_state(rt_s) + mm(a_rb, u) + mm(a_rk, v_t)
        y = jnp.zeros((c, gl), F32)
        for g in range(gsz):
            y = jnp.where(lane_head == g, y_st[g * c:(g + 1) * c, :], y)
        y_ref[:, sl] = y
        pcol = jnp.sum(jnp.where(r2 == c2, jnp.broadcast_to(jnp.exp(clast), (gl, gl)), 0.0),
                       axis=1, keepdims=True)
        upd = mm_tn(jnp.where(mine, tile(bvec * ef), 0.0), u) + mm_tn(kmod * ef, v)
        upd = jnp.where(head_diag, upd, 0.0)
        s_scr[2 * gi] = s_lo * pcol[:LANES, :] + upd[:LANES, :LANES]
        s_scr[2 * gi + 1] = s_hi * pcol[LANES:, :] + upd[LANES:, LANES:]

    @pl.when(ch == nc - 1)
    def _():
        sout_ref[0] = s_scr[...]


def _rw_chunk(r, lw, km, v, kk, bv, s0, *, bsz, tp, c, fine):
    nc = tp // c
    n = bsz * tp
    row = lambda b, j: (b * nc + j, 0)
    blk = pl.BlockSpec((c, D_MODEL), row)
    st = pl.BlockSpec((1, RW_HEADS // 2, LANES, LANES), lambda b, j: (b, 0, 0, 0))
    body = functools.partial(_rw_chunk_body, c=c, nc=nc, fine=fine)
    return pl.pallas_call(
        body, grid=(bsz, nc),
        in_specs=[blk] * 6 + [st],
        out_specs=[blk, st],
        out_shape=[_sds((n, D_MODEL)), _sds((bsz, RW_HEADS // 2, LANES, LANES))],
        scratch_shapes=[pltpu.VMEM((RW_HEADS // 2, LANES, LANES), F32)],
        compiler_params=_cparams(("parallel", "arbitrary")), name="rw_chunk",
    )(r, lw, km, v, kk, bv, s0)


def _rw_out_body(y_ref, r_ref, km_ref, v_ref, g_ref, x_ref, vec_ref, bd_ref, w_ref, o_ref):
    y = y_ref[...]
    inv = 1.0 / RW_HEAD
    mean = _dot_exact_rhs(y, bd_ref[...]) * inv
    d = y - mean
    var = _dot_exact_rhs(d * d, bd_ref[...]) * inv
    yn = d * lax.rsqrt(var + RW_GN_EPS) * vec_ref[4:5, :] + vec_ref[5:6, :]
    bonus = _dot_exact_rhs(r_ref[...] * km_ref[...] * vec_ref[6:7, :], bd_ref[...]) * v_ref[...]
    o_ref[...] = x_ref[...] + _dot(((yn + bonus) * g_ref[...]).astype(BF16), w_ref[...])


def _rw_out(y, r, km, v, g, x, vec, bd, w_out, tm):
    n = y.shape[0]
    blk = pl.BlockSpec((tm, D_MODEL), lambda i: (i, 0))
    return pl.pallas_call(
        _rw_out_body, grid=(n // tm,),
        in_specs=[blk] * 6 + [_full(vec), _full(bd), _full(w_out)],
        out_specs=blk, out_shape=_sds((n, D_MODEL)),
        compiler_params=_cparams(("parallel",)), name="rw_out",
    )(y, r, km, v, g, x, vec, bd, w_out)


def _rwkv_layer(x, shift0, s0, p, *, bsz, tp, tt, t_valid, chunk, fine, tm):
    r, lw, km, v, kk, bv, g, hl = _rw_pre(x, shift0, p, bsz=bsz, tp=tp, tt=tt, t_valid=t_valid)
    y, s_new = _rw_chunk(r, lw, km, v, kk, bv, s0, bsz=bsz, tp=tp, c=chunk, fine=fine)
    out = _rw_out(y, r, km, v, g, x, p["vec"], p["bd"], p["w_out"], tm)
    return out, hl[:, SUBLANES - 1, :], s_new


def _top_values(arrs, dsts, k):
    idxs = [_iota2(a.shape, 0) for a in arrs]

    def step(i, carry):
        out = []
        for s, dst, idx in zip(carry, dsts, idxs):
            m = jnp.max(s, axis=0, keepdims=True)
            dst[pl.ds(i, 1), :] = m
            first = jnp.min(jnp.where(s == m, idx, s.shape[0]), axis=0, keepdims=True)
            out.append(jnp.where(idx == first, NEG_BIG, s))
        return tuple(out)

    lax.fori_loop(0, k, step, tuple(arrs))


def _pair_candidates(top1, top2):
    t2_8 = top2[0:SUBLANES, :]
    sub = _iota2(t2_8.shape, 0)
    pieces = [top1[0:1, :] + top2[...], top1[1:2, :] + t2_8]
    for a in range(2, SUBLANES):
        pieces.append(jnp.where(sub < PEER_TOPK // (a + 1), top1[a:a + 1, :] + t2_8, NEG_BIG))
    pieces.append(top1[SUBLANES:PEER_TOPK, :] + top2[0:1, :])
    return jnp.concatenate(pieces, axis=0)


def _peer_route_body(x_ref, ln_ref, wqh_ref, wql_ref, skh_ref, skl_ref,
                     xn_ref, s1_ref, s2_ref, e1_ref, e2_ref, thr_ref,
                     qh_s, ql_s, top_s, best_s):
    xn = _rms(x_ref[...], ln_ref[...])
    xn_ref[...] = xn.astype(BF16)
    xh, xl = _split(xn)
    q_t = _dot_nt(wqh_ref[...], xh) + (_dot_nt(wqh_ref[...], xl) + _dot_nt(wql_ref[...], xh))
    qh, ql = _split(q_t)
    qh_s[...] = qh
    ql_s[...] = ql

    def head_pair(hp, _):
        scores = []
        for e in range(2):
            for p, s_ref in enumerate((s1_ref, s2_ref)):
                idx = 2 * (2 * hp + e) + p
                r0 = pl.multiple_of(idx * PEER_HALF, PEER_HALF)
                skh = skh_ref[idx]
                qhh = qh_s[pl.ds(r0, PEER_HALF), :]
                s = _dot(skh, qhh) + (_dot(skh, ql_s[pl.ds(r0, PEER_HALF), :]) + _dot(skl_ref[idx], qhh))
                s_ref[2 * hp + e] = s
                scores.append(s)
        _top_values(scores, [top_s.at[i] for i in range(4)], PEER_TOPK)
        cands = [_pair_candidates(top_s.at[2 * e], top_s.at[2 * e + 1]) for e in range(2)]
        _top_values(cands, [best_s.at[e] for e in range(2)], PEER_TOPK)
        for e in range(2):
            h = 2 * hp + e
            best = best_s[e]
            zsum = jnp.sum(jnp.exp(best - best[0:1, :]), axis=0, keepdims=True)
            thr_ref[pl.ds(h, 1), :] = best[PEER_TOPK - 1:PEER_TOPK, :]
            e1_ref[h] = jnp.exp(s1_ref[h] - top_s[2 * e, 0:1, :]) / zsum
            e2_ref[h] = jnp.exp(s2_ref[h] - top_s[2 * e + 1, 0:1, :])
        return 0

    lax.fori_loop(0, PEER_HEADS // 2, head_pair, 0)


def _peer_route(x, ln, wq_hi, wq_lo, sk_hi, sk_lo, tm):
    n = x.shape[0]
    body = _peer_route_body
    sc = pl.BlockSpec((PEER_HEADS, PEER_NKEYS, tm), lambda i: (0, 0, i))
    return pl.pallas_call(
        body, grid=(n // tm,),
        scratch_shapes=[pltpu.VMEM((D_MODEL, tm), BF16), pltpu.VMEM((D_MODEL, tm), BF16),
                        pltpu.VMEM((4, PEER_TOPK, tm), F32), pltpu.VMEM((2, PEER_TOPK, tm), F32)],
        in_specs=[pl.BlockSpec((tm, D_MODEL), lambda i: (i, 0)), _full(ln), _full(wq_hi), _full(wq_lo),
                  _full(sk_hi), _full(sk_lo)],
        out_specs=[pl.BlockSpec((tm, D_MODEL), lambda i: (i, 0)), sc, sc, sc, sc,
                   pl.BlockSpec((PEER_HEADS, tm), lambda i: (0, i))],
        out_shape=[_sds((n, D_MODEL), BF16)] + [_sds((PEER_HEADS, PEER_NKEYS, n))] * 4
                  + [_sds((PEER_HEADS, n))],
        compiler_params=_cparams(("parallel",)), name="peer_route",
    )(x, ln, wq_hi, wq_lo, sk_hi, sk_lo)


def _gelu(a):
    return 0.5 * a * (1.0 + lax.erf(a * (2.0 ** -0.5)))


def _peer_gate(act_ref, p_ref, s1_ref, s2_ref, e1_ref, e2_ref, thr_ref, base, lane_blocks, *, ncb):
    for lb in lane_blocks:
        ls = slice(lb * LANES, (lb + 1) * LANES)
        s1t =[s1_ref[h, pl.ds(base, ncb), ls] for h in range(PEER_HEADS)]
        e1t = [e1_ref[h, pl.ds(base, ncb), ls] for h in range(PEER_HEADS)]
        thr = thr_ref[:, ls]
        for cc in range(ncb):
            gate = jnp.zeros((PEER_NKEYS, LANES), F32)
            for h in range(PEER_HEADS):
                ssum = s1t[h][cc:cc + 1, :] + s2_ref[h, :, ls]
                sel = jnp.where(ssum >= thr[h:h + 1, :], e2_ref[h, :, ls], 0.0)
                gate = gate + sel * e1t[h][cc:cc + 1, :]
            rs = slice(cc * PEER_NKEYS, (cc + 1) * PEER_NKEYS)
            p_ref[rs, ls] = (gate * _gelu(act_ref[rs, ls])).astype(BF16)


def _peer_main_body(xn_ref, x_ref, s1_ref, s2_ref, e1_ref, e2_ref, thr_ref, u0_ref, ub_ref, una_ref,
                    vt_ref, o_ref, act_a, act_b, p_a, p_b, acc_s, *, tm, te):
    j = pl.program_id(1)
    ncb = te // PEER_NKEYS
    base_a = pl.multiple_of(j * (2 * ncb), SUBLANES)
    base_b = pl.multiple_of(j * (2 * ncb) + ncb, SUBLANES)
    gate = functools.partial(_peer_gate, s1_ref=s1_ref, s2_ref=s2_ref, e1_ref=e1_ref, e2_ref=e2_ref,
                             thr_ref=thr_ref, ncb=ncb)
    tc = min(tm, MXU_DIM)
    chunks = [(slice(c0, c0 + tc), range(c0 // LANES, (c0 + tc) // LANES)) for c0 in range(0, tm, tc)]

    @pl.when(j == 0)
    def _():
        acc_s[...] = jnp.zeros_like(acc_s)
        act_a[...] = _dot_nt(u0_ref[...], xn_ref[...])

    for ts, lbs in chunks:
        act_b[:, ts] = _dot_nt(ub_ref[...], xn_ref[ts, :])
        gate(act_a, p_a, base=base_a, lane_blocks=lbs)
    for ts, lbs in chunks:
        acc_s[:, ts] += _dot(vt_ref[:, 0:te], p_a[:, ts])
        act_a[:, ts] = _dot_nt(una_ref[...], xn_ref[ts, :])
        gate(act_b, p_b, base=base_b, lane_blocks=lbs)
    for ts, lbs in chunks:
        acc_s[:, ts] += _dot(vt_ref[:, te:2 * te], p_b[:, ts])

    @pl.when(j == pl.num_programs(1) - 1)
    def _():
        o_ref[...] = x_ref[...] + acc_s[...].T


def _peer_main(xn, x, s1, s2, e1, e2, thr, u_bf16, vt_bf16, tm, te):
    n = x.shape[0]
    n_exp = u_bf16.shape[0]
    assert te % (SUBLANES * PEER_NKEYS) == 0 and n_exp % (2 * te) == 0 and n % tm == 0
    nj = n_exp // (2 * te)
    body = functools.partial(_peer_main_body, tm=tm, te=te)
    once = pl.Buffered(1)
    sc = pl.BlockSpec((PEER_HEADS, PEER_NKEYS, tm), lambda i, j: (0, 0, i), pipeline_mode=once)
    tok = pl.BlockSpec((tm, D_MODEL), lambda i, j: (i, 0))
    return pl.pallas_call(
        body, grid=(n // tm, nj),
        in_specs=[tok, tok, sc, sc, sc, sc,
                  pl.BlockSpec((PEER_HEADS, tm), lambda i, j: (0, i)),
                  pl.BlockSpec((te, D_MODEL), lambda i, j: (0, 0), pipeline_mode=once),
                  pl.BlockSpec((te, D_MODEL), lambda i, j: (2 * j + 1, 0)),
                  pl.BlockSpec((te, D_MODEL), lambda i, j: (jnp.minimum(2 * j + 2, 2 * nj - 1), 0)),
                  pl.BlockSpec((D_MODEL, 2 * te), lambda i, j: (0, j))],
        out_specs=pl.BlockSpec((tm, D_MODEL), lambda i, j: (i, 0)), out_shape=_sds((n, D_MODEL)),
        scratch_shapes=[pltpu.VMEM((te, tm), F32), pltpu.VMEM((te, tm), F32),
                        pltpu.VMEM((te, tm), BF16), pltpu.VMEM((te, tm), BF16),
                        pltpu.VMEM((D_MODEL, tm), F32)],
        compiler_params=_cparams(("parallel", "arbitrary")), name="peer_main",
    )(xn, x, s1, s2, e1, e2, thr, u_bf16, u_bf16, u_bf16, vt_bf16)


def _peer_layer(x, p, tm, te):
    xn, s1, s2, e1, e2, thr = _peer_route(x, p["ln"], p["wq_hi"], p["wq_lo"], p["sk_hi"], p["sk_lo"],
                                          min(tm, 256))
    return _peer_main(xn, x, s1, s2, e1, e2, thr, p["u"], p["vt"], tm, te)


def _row(v):
    return v.reshape(1, -1).astype(F32)


def _pad_rows(rows, total=SUBLANES):
    m = jnp.stack(rows).astype(F32)
    return jnp.pad(m, ((0, total - m.shape[0]), (0, 0)))


def _blockdiag_ones(width, block):
    i = jnp.arange(width) // block
    return (i[:, None] == i[None, :]).astype(BF16)


def _prep_dn(j, ln, dn_w_in, dn_conv_w, dn_A_log, dn_dt_bias, dn_o_norm, dn_w_out):
    pad8 = lambda v: jnp.pad(v.astype(F32), (0, LANES - DN_HEADS))
    return dict(
        ln=_row(ln),
        w_in=jnp.pad(dn_w_in[j], ((0, 0), (0, DN_PROJ_PAD - DN_PROJ))).astype(BF16),
        conv_w=_pad_rows(list(dn_conv_w[j])),
        pa=_pad_rows([pad8(dn_A_log[j]), pad8(dn_dt_bias[j])]),
        o_norm=_row(dn_o_norm[j]),
        w_out=dn_w_out[j].astype(BF16))


def _prep_rw(j, ln, rw):
    (mu, w_r, w_k, w_v, w0, w1, w2, a0, a1, a2, g1, g2, k_k, k_a, r_k, lnx_w, lnx_b, w_out) = rw
    bf = lambda w: w[j].astype(BF16)
    return dict(
        ln=_row(ln), mu=_pad_rows(list(mu[j])),
        w_r=bf(w_r), w_k=bf(w_k), w_v=bf(w_v), w1=bf(w1), w2=bf(w2), a1=bf(a1), a2=bf(a2),
        g1=bf(g1), g2=bf(g2),
        vec=_pad_rows([w0[j], a0[j], k_k[j], k_a[j], lnx_w[j], lnx_b[j], r_k[j].reshape(-1)]),
        bd=_blockdiag_ones(D_MODEL, RW_HEAD), w_out=bf(w_out))


def _prep_peer(i, ln, peer_w_q, peer_subkeys, peer_u, peer_v):
    wq_t = peer_w_q[i].T
    wq_hi = wq_t.astype(BF16)
    wq_lo = (wq_t - wq_hi.astype(F32)).astype(BF16)
    sk = peer_subkeys[i].reshape(2 * PEER_HEADS, PEER_NKEYS, PEER_HALF)
    sk_hi = sk.astype(BF16)
    sk_lo = (sk - sk_hi.astype(F32)).astype(BF16)
    return dict(ln=_row(ln), wq_hi=wq_hi, wq_lo=wq_lo, sk_hi=sk_hi, sk_lo=sk_lo,
                u=peer_u[i].astype(BF16), vt=peer_v[i].T.astype(BF16))


def _pair_states(s):
    b = s.shape[0]
    st = jnp.swapaxes(s, -1, -2).reshape(b, RW_HEADS // 2, 2, RW_HEAD, RW_HEAD)
    z = jnp.zeros_like(st[:, :, 0])
    top = jnp.concatenate([st[:, :, 0], z], axis=-1)
    bot = jnp.concatenate([z, st[:, :, 1]], axis=-1)
    return jnp.concatenate([top, bot], axis=-2)


def _unpair_states(sp):
    b = sp.shape[0]
    h0 = sp[:, :, :RW_HEAD, :RW_HEAD]
    h1 = sp[:, :, RW_HEAD:, RW_HEAD:]
    st = jnp.stack([h0, h1], axis=2).reshape(b, RW_HEADS, RW_HEAD, RW_HEAD)
    return jnp.swapaxes(st, -1, -2)


def _trunk(x, dn_state, dn_conv, rw_state, rw_shift, sb_past, params, cfg):
    bsz, tp, t_valid = cfg["bsz"], cfg["tp"], cfg["t_valid"]
    tm = cfg["tm"]
    outs = dict(dn_s=[], dn_c=[], k=[], v=[], rw_s=[], rw_x=[])
    for i in range(4):
        j = i // 3
        kind = i % 3
        if kind == 0:
            c0 = jnp.pad(dn_conv[j], ((0, 0), (SUBLANES - 3, 0), (0, 0)))
            x, tail, s_new = _delta_layer(x, c0, dn_state[j], params["dn"][j], bsz=bsz, tp=tp,
                                          tt=cfg["tt"], t_valid=t_valid, chunk=cfg["dn_chunk"],
                                          fine=cfg["fine"], tm=tm)
            outs["dn_c"].append(tail)
            outs["dn_s"].append(s_new)
        elif kind == 1:
            p = params["sb"][j]
            q, k, v, qb, kb, vb = _sb_qkv(x, p["ln"], p["w_qkv"], tm)
            if sb_past is None:
                o = _sb_attn(qb, kb, vb, p["bias"], bsz=bsz, t=tp, tq=cfg["tq"], tk=cfg["tk"], qs=cfg["qs"])
            else:
                cache_k, cache_v, page_table = sb_past
                q3 = q.reshape(bsz, tp, D_MODEL)[:, :t_valid]
                hm = p["headmask"]
                qbd = (q3[:, :, None, :] * hm[None, None]).reshape(bsz, t_valid * SB_HEADS, D_MODEL)
                padk = lambda a: jnp.pad(a.reshape(bsz, tp, D_MODEL), ((0, 0), (0, PAGE_SIZE - tp), (0, 0)))
                feat_major = lambda c: jnp.transpose(c, (0, 2, 3, 1)).reshape(-1, D_MODEL, PAGE_SIZE)
                o = _sb_decode(qbd, p["biasrows"], jnp.tile(hm, (t_valid, 1)), padk(k), padk(v),
                               feat_major(cache_k[j]), feat_major(cache_v[j]), page_table,
                               t_new=t_valid, ppb=4)
                o = o.reshape(bsz * tp, D_MODEL)
            x = _mm_res(o, x, p["w_out"], tm)
            outs["k"].append(k.reshape(bsz, tp, SB_HEADS, SB_DH)[:, :t_valid])
            outs["v"].append(v.reshape(bsz, tp, SB_HEADS, SB_DH)[:, :t_valid])
        else:
            sh0 = jnp.pad(rw_shift[j][:, None, :], ((0, 0), (SUBLANES - 1, 0), (0, 0)))
            x, hl, s_new = _rwkv_layer(x, sh0, _pair_states(rw_state[j]), params["rw"][j], bsz=bsz,
                                       tp=tp, tt=cfg["tt"], t_valid=t_valid, chunk=cfg["rw_chunk"],
                                       fine=cfg["fine"], tm=tm)
            outs["rw_x"].append(hl)
            outs["rw_s"].append(_unpair_states(s_new))
        x = _peer_layer(x, params["peer"][i], cfg["peer_tm"], cfg["peer_te"])
    y = _final_norm(x, params["ln_final"], tm)
    return y, outs


def kernel(x_prompt, x_sample, state_dn, state_dn_conv, cache_k, cache_v, page_table, state_wkv, state_shift, ln_mix, ln_ffn, ln_final, dn_w_in, dn_conv_w, dn_A_log, dn_dt_bias, dn_o_norm, dn_w_out, sb_w_qkv, sb_bias, sb_w_out, rw_mu, rw_w_r, rw_w_k, rw_w_v, rw_w0, rw_w1, rw_w2, rw_a0, rw_a1, rw_a2, rw_g1, rw_g2, rw_k_k, rw_k_a, rw_r_k, rw_lnx_w, rw_lnx_b, rw_w_out, peer_w_q, peer_subkeys, peer_u, peer_v):
    bsz, seq, _ = x_prompt.shape
    dbsz, dseq, _ = x_sample.shape
    dtp = SUBLANES
    rw = (rw_mu, rw_w_r, rw_w_k, rw_w_v, rw_w0, rw_w1, rw_w2, rw_a0, rw_a1, rw_a2, rw_g1, rw_g2,
          rw_k_k, rw_k_a, rw_r_k, rw_lnx_w, rw_lnx_b, rw_w_out)
    headmask = (jnp.arange(D_MODEL)[None, :] // SB_DH == jnp.arange(SB_HEADS)[:, None]).astype(F32)
    params = dict(
        dn=[_prep_dn(j, ln_mix[3 * j], dn_w_in, dn_conv_w, dn_A_log, dn_dt_bias, dn_o_norm, dn_w_out)
            for j in range(2)],
        sb=[dict(ln=_row(ln_mix[1]), w_qkv=sb_w_qkv[0].astype(BF16), bias=sb_bias[0].astype(F32),
                 biasrows=jnp.broadcast_to(jnp.tile(sb_bias[0].astype(F32), dseq)[:, None],
                                           (dseq * SB_HEADS, LANES)),
                 headmask=headmask, w_out=sb_w_out[0].astype(BF16))],
        rw=[_prep_rw(0, ln_mix[2], rw)],
        peer=[_prep_peer(i, ln_ffn[i], peer_w_q, peer_subkeys, peer_u, peer_v) for i in range(4)],
        ln_final=_row(ln_final))

    cfg_p = dict(bsz=bsz, tp=seq, t_valid=seq, tt=256, tm=256, dn_chunk=DN_CHUNK, rw_chunk=RW_CHUNK,
                 fine=False, tq=512, tk=512, qs=128, peer_tm=512, peer_te=1024)
    zeros = lambda *s: jnp.zeros(s, F32)
    y_p, o_p = _trunk(x_prompt.reshape(bsz * seq, D_MODEL),
                      zeros(2, bsz, DN_HEADS, DN_DK, DN_DK), zeros(2, bsz, DN_CONV - 1, 3 * DN_WIDTH),
                      zeros(1, bsz, RW_HEADS, RW_HEAD, RW_HEAD), zeros(1, bsz, D_MODEL), None,
                      params, cfg_p)

    cfg_s = dict(bsz=dbsz, tp=dtp, t_valid=dseq, tt=dtp, tm=dbsz * dtp, dn_chunk=dtp, rw_chunk=dtp,
                 fine=True, peer_tm=dbsz * dtp, peer_te=1024)
    xs = jnp.pad(x_sample, ((0, 0), (0, dtp - dseq), (0, 0))).reshape(dbsz * dtp, D_MODEL)
    y_s, o_s = _trunk(xs, state_dn, state_dn_conv, state_wkv, state_shift,
                      (cache_k, cache_v, page_table), params, cfg_s)

    def pack(y, o, b, tp, t):
        return (y.reshape(b, tp, D_MODEL)[:, :t], jnp.stack(o["dn_s"]), jnp.stack(o["dn_c"]),
                jnp.stack(o["k"]), jnp.stack(o["v"]), jnp.stack(o["rw_s"]), jnp.stack(o["rw_x"]))

    pp = pack(y_p, o_p, bsz, seq, seq)
    ps = pack(y_s, o_s, dbsz, dtp, dseq)
    return (pp[0], ps[0]) + pp[1:] + ps[1:]
```
